```python
import math
import jax, jax.numpy as jnp
from jax import lax
import numpy as np

D_MODEL = 1024
BATCH = 1
SEQ = 16384
DEPTH = 1
DEC_BATCH = 16
DEC_SEQ = 2048
PAST_LEN = 128

BLK = 128
WINDOW = 128
HA = 8
HKV_A = 2
G_A = HA // HKV_A
DH_A = 64
WIDTH_A = HA * DH_A
NUM_BUCKETS = 32
MAX_DISTANCE = 128
HB = 8
Q_LORA = 256
KV_LORA = 128
NOPE = 64
ROPE = 32
DV = 64
WIDTH_B = HB * DV
ROPE_THETA = 10000.0
EPS = 1e-6
SPLITS = [WIDTH_A, HKV_A * DH_A, HKV_A * DH_A, WIDTH_A, Q_LORA, KV_LORA, ROPE, WIDTH_B, D_MODEL, D_MODEL]
IN_COLS = sum(SPLITS)

kernel_name = "hybrid_wingqa_mla_gated_encoder"


def rmsnorm(x, g):
    xf = x.astype(jnp.float32)
    y = xf * lax.rsqrt(jnp.mean(xf * xf, axis=-1, keepdims=True) + EPS)
    return (y * g.astype(jnp.float32)).astype(x.dtype)


def t5_bucket(rel):
    nb = NUM_BUCKETS // 2
    max_exact = nb // 2
    ret = (rel > 0).astype(jnp.int32) * nb
    n = jnp.abs(rel)
    nf = jnp.maximum(n, 1).astype(jnp.float32)
    large = max_exact + (jnp.log(nf / max_exact) / math.log(MAX_DISTANCE / max_exact)
                         * (nb - max_exact)).astype(jnp.int32)
    large = jnp.minimum(large, nb - 1)
    return ret + jnp.where(n < max_exact, n, large)


def rope(t):
    S, R = t.shape[1], t.shape[-1]
    half = R // 2
    inv = jnp.power(ROPE_THETA, -jnp.arange(half, dtype=jnp.float32) / half)
    ang = jnp.arange(S, dtype=jnp.float32)[:, None] * inv[None, :]
    cos = jnp.cos(ang)[None, :, None, :]
    sin = jnp.sin(ang)[None, :, None, :]
    tf = t.astype(jnp.float32)
    t1, t2 = tf[..., :half], tf[..., half:]
    return jnp.concatenate([t1 * cos - t2 * sin, t1 * sin + t2 * cos], axis=-1).astype(t.dtype)


def window_gqa(q, k, v, sink, rel_bias):
    B, S = q.shape[0], q.shape[1]
    nb = S // BLK
    qb = q.reshape(B, nb, BLK, HKV_A, G_A, DH_A)

    def windows(t):
        tp = jnp.pad(t, ((0, 0), (BLK, BLK), (0, 0), (0, 0))).reshape(B, nb + 2, BLK, HKV_A, DH_A)
        return jnp.concatenate([tp[:, :-2], tp[:, 1:-1], tp[:, 2:]], axis=2)

    kw, vw = windows(k), windows(v)
    s = jnp.einsum('bnqhgd,bnkhd->bnhgqk', qb, kw,
                   preferred_element_type=jnp.float32) * (DH_A ** -0.5)
    qi = jnp.arange(BLK, dtype=jnp.int32)[:, None]
    kj = jnp.arange(3 * BLK, dtype=jnp.int32)[None, :]
    rel = kj - BLK - qi
    bias = rel_bias.astype(jnp.float32)[t5_bucket(rel)]
    bias = bias.transpose(2, 0, 1).reshape(HKV_A, G_A, BLK, 3 * BLK)
    kabs = jnp.arange(nb, dtype=jnp.int32)[:, None] * BLK + kj - BLK
    valid = (jnp.abs(rel) <= WINDOW)[None] & ((kabs >= 0) & (kabs < S))[:, None, :]
    s = jnp.where(valid[None, :, None, None], s + bias[None, None], -1e30)
    sk = sink.astype(jnp.float32).reshape(HKV_A, G_A)[:, :, None]
    m = jnp.maximum(jnp.max(s, axis=-1), sk)
    p = jnp.exp(s - m[..., None])
    denom = jnp.sum(p, axis=-1) + jnp.exp(sk - m)
    p = (p / denom[..., None]).astype(v.dtype)
    o = jnp.einsum('bnhgqk,bnkhd->bnqhgd', p, vw)
    return o.reshape(B, S, WIDTH_A)


def mla_attn(qn, qr, kn, kr, v):
    B, S = qn.shape[0], qn.shape[1]
    nb = S // BLK
    scale = (NOPE + ROPE) ** -0.5

    def to_blocks(t):
        return t.reshape(B, nb, BLK, *t.shape[2:]).swapaxes(0, 1)

    def one(blk):
        qn_b, qr_b = blk
        s = (jnp.einsum('bqhd,bkhd->bhqk', qn_b, kn, preferred_element_type=jnp.float32)
             + jnp.einsum('bqhr,bkr->bhqk', qr_b, kr, preferred_element_type=jnp.float32)) * scale
        p = jax.nn.softmax(s, axis=-1).astype(v.dtype)
        return jnp.einsum('bhqk,bkhd->bqhd', p, v)

    o = lax.map(one, (to_blocks(qn), to_blocks(qr)))
    return o.swapaxes(0, 1).reshape(B, S, WIDTH_B)


def hybrid_layer(x, ln_g, w_in, b_gate, sink_a, q_norm_g, kv_norm_g, w_uq, w_ukv,
                 w_proj_a, w_proj_b, w_out, rel_bias):
    B, S, _ = x.shape
    h = rmsnorm(x, ln_g)
    u = h @ w_in
    idx = [sum(SPLITS[:i + 1]) for i in range(len(SPLITS) - 1)]
    qa, ka, va, za, cq, ckv, kr, zb, ga, gb = jnp.split(u, idx, axis=-1)
    ya = window_gqa(qa.reshape(B, S, HA, DH_A), ka.reshape(B, S, HKV_A, DH_A),
                    va.reshape(B, S, HKV_A, DH_A), sink_a, rel_bias)
    oa = (ya * jax.nn.silu(za)) @ w_proj_a
    qf = (rmsnorm(cq, q_norm_g) @ w_uq).reshape(B, S, HB, NOPE + ROPE)
    qn, qr = qf[..., :NOPE], rope(qf[..., NOPE:])
    kvf = (rmsnorm(ckv, kv_norm_g) @ w_ukv).reshape(B, S, HB, NOPE + DV)
    kn, vb = kvf[..., :NOPE], kvf[..., NOPE:]
    krr = rope(kr[:, :, None, :])[:, :, 0, :]
    yb = mla_attn(qn, qr, kn, krr, vb)
    ob = (yb * jax.nn.silu(zb)) @ w_proj_b
    merged = jax.nn.sigmoid(ga + b_gate[:D_MODEL]) * oa + jax.nn.sigmoid(gb + b_gate[D_MODEL:]) * ob
    return x + merged @ w_out


def setup_inputs(seed: int = 0) -> dict:
    key = jax.random.key(seed)
    ks = jax.random.split(key, 16)
    f = jnp.float32
    nrm = lambda k, shape, s: jax.random.normal(k, shape, f) * s
    return {
        "x_prompt": nrm(ks[0], (BATCH, SEQ, D_MODEL), 1.0),
        "x_sample": nrm(ks[1], (DEC_BATCH, DEC_SEQ, D_MODEL), 1.0),
        "ln_g": 1.0 + nrm(ks[2], (DEPTH, D_MODEL), 0.02),
        "w_in": nrm(ks[3], (DEPTH, D_MODEL, IN_COLS), D_MODEL ** -0.5),
        "b_gate": nrm(ks[4], (DEPTH, 2 * D_MODEL), 0.1),
        "sink_a": nrm(ks[5], (DEPTH, HA), 0.5),
        "q_norm_g": 1.0 + nrm(ks[6], (DEPTH, Q_LORA), 0.02),
        "kv_norm_g": 1.0 + nrm(ks[7], (DEPTH, KV_LORA), 0.02),
        "w_uq": nrm(ks[8], (DEPTH, Q_LORA, HB * (NOPE + ROPE)), Q_LORA ** -0.5),
        "w_ukv": nrm(ks[9], (DEPTH, KV_LORA, HB * (NOPE + DV)), KV_LORA ** -0.5),
        "w_proj_a": nrm(ks[10], (DEPTH, WIDTH_A, D_MODEL), WIDTH_A ** -0.5),
        "w_proj_b": nrm(ks[11], (DEPTH, WIDTH_B, D_MODEL), WIDTH_B ** -0.5),
        "w_out": nrm(ks[12], (DEPTH, D_MODEL, D_MODEL), D_MODEL ** -0.5),
        "rel_bias": nrm(ks[13], (NUM_BUCKETS, HA), 0.3),
        "final_g": 1.0 + nrm(ks[14], (D_MODEL,), 0.02),
    }


def reference(x_prompt, x_sample, ln_g, w_in, b_gate, sink_a, q_norm_g, kv_norm_g, w_uq, w_ukv,
              w_proj_a, w_proj_b, w_out, rel_bias, final_g):
    def trunk(x):
        for l in range(DEPTH):
            x = hybrid_layer(x, ln_g[l], w_in[l], b_gate[l], sink_a[l], q_norm_g[l], kv_norm_g[l],
                             w_uq[l], w_ukv[l], w_proj_a[l], w_proj_b[l], w_out[l], rel_bias)
        return rmsnorm(x, final_g)

    y_prompt = trunk(x_prompt)
    y_sample = trunk(x_sample)
    return (y_prompt, y_sample)
```

```python
import functools
import math

import jax
import jax.numpy as jnp
from jax import lax
from jax.experimental import pallas as pl
from jax.experimental.pallas import tpu as pltpu

D_MODEL = 1024
BLK = 128
WINDOW = 128
HA = 8
HKV_A = 2
G_A = HA // HKV_A
DH_A = 64
WIDTH_A = HA * DH_A
NUM_BUCKETS = 32
MAX_DISTANCE = 128
HB = 8
Q_LORA = 256
KV_LORA = 128
NOPE = 64
ROPE = 32
DV = 64
WIDTH_B = HB * DV
ROPE_THETA = 10000.0
EPS = 1e-6
NEG = -1e30

LANES = 128
VMEM_LIMIT = 56 * 1024 * 1024

F32 = jnp.float32
BF16 = jnp.bfloat16

_W1_GROUPS = (("qa", HA * LANES), ("ka", HKV_A * LANES), ("va", 2 * HKV_A * LANES), ("za", WIDTH_A),
              ("cq", Q_LORA), ("ckv", KV_LORA), ("kr", LANES), ("zb", WIDTH_B),
              ("ga", D_MODEL), ("gb", D_MODEL))
_W1_OFF = {}
_o = 0
for _n, _w in _W1_GROUPS:
    _W1_OFF[_n] = (_o, _o + _w)
    _o += _w
W1_COLS = _o


def _pad_heads(w, n_heads, width, lane_off=0):
    k = w.shape[0]
    w = w.reshape(k, n_heads, width)
    out = jnp.zeros((k, n_heads, LANES), w.dtype)
    out = out.at[:, :, lane_off:lane_off + width].set(w)
    return out.reshape(k, n_heads * LANES)


def _pack_weights(w_in, w_uq, w_ukv):
    idx = [0]
    for s in (WIDTH_A, HKV_A * DH_A, HKV_A * DH_A, WIDTH_A, Q_LORA, KV_LORA, ROPE, WIDTH_B, D_MODEL, D_MODEL):
        idx.append(idx[-1] + s)
    qa, ka, va, za, cq, ckv, kr, zb, ga, gb = [w_in[:, idx[i]:idx[i + 1]] for i in range(10)]
    va_lo = _pad_heads(va, HKV_A, DH_A, 0)
    va_hi = _pad_heads(va, HKV_A, DH_A, DH_A)
    kr_p = jnp.zeros((D_MODEL, LANES), w_in.dtype).at[:, NOPE:NOPE + ROPE].set(kr)
    w1 = jnp.concatenate([_pad_heads(qa, HA, DH_A), _pad_heads(ka, HKV_A, DH_A), va_lo, va_hi, za,
                          cq, ckv, kr_p, zb, ga, gb], axis=1).astype(BF16)
    wq = _pad_heads(w_uq, HB, NOPE + ROPE).astype(BF16)
    kvf = w_ukv.reshape(KV_LORA, HB, NOPE + DV)
    wk = _pad_heads(kvf[:, :, :NOPE].reshape(KV_LORA, HB * NOPE), HB, NOPE).astype(BF16)
    wv = _pad_heads(kvf[:, :, NOPE:].reshape(KV_LORA, HB * DV), HB, DV).astype(BF16)
    return w1, wq, wk, wv


def _rope_tables(seq):
    half = ROPE // 2
    inv = jnp.power(ROPE_THETA, -jnp.arange(half, dtype=F32) / half)
    ang = jnp.arange(seq, dtype=F32)[:, None] * inv[None, :]
    cos, sin = jnp.cos(ang), jnp.sin(ang)
    z = jnp.zeros((seq, half), F32)
    pad = jnp.zeros((seq, LANES - NOPE - ROPE), F32)
    ones = jnp.ones((seq, NOPE), F32)
    zn = jnp.zeros((seq, NOPE), F32)
    cosq = jnp.concatenate([ones, cos, cos, pad], axis=1)
    sin_lo = jnp.concatenate([zn, -sin, z, pad], axis=1)
    sin_hi = jnp.concatenate([zn, z, sin, pad], axis=1)
    return cosq, sin_lo, sin_hi


def _bucket_table():
    qi = jnp.arange(BLK, dtype=jnp.int32)[:, None]
    kj = jnp.arange(3 * BLK, dtype=jnp.int32)[None, :]
    rel = kj - BLK - qi
    nb = NUM_BUCKETS // 2
    max_exact = nb // 2
    ret = (rel > 0).astype(jnp.int32) * nb
    n = jnp.abs(rel)
    nf = jnp.maximum(n, 1).astype(F32)
    large = max_exact + (jnp.log(nf / max_exact) / math.log(MAX_DISTANCE / max_exact)
                         * (nb - max_exact)).astype(jnp.int32)
    large = jnp.minimum(large, nb - 1)
    bucket = ret + jnp.where(n < max_exact, n, large)
    return jnp.where(n <= WINDOW, bucket, -1)


def _rms(v, g):
    return v * lax.rsqrt(jnp.mean(v * v, axis=-1, keepdims=True) + EPS) * g


def _sigmoid(v):
    return 1.0 / (1.0 + jnp.exp(-v))


def _rot(v, cosq, sin_lo, sin_hi):
    return v * cosq + pltpu.roll(v, LANES - ROPE // 2, 1) * sin_lo + pltpu.roll(v, ROPE // 2, 1) * sin_hi


def _proj_kernel(x_ref, lng_ref, w1_ref, bg_ref, qg_ref, kvg_ref, wq_ref, wk_ref, wv_ref,
                 cos_ref, slo_ref, shi_ref,
                 qa_ref, ka_ref, va_ref, sza_ref, szb_ref, gga_ref, ggb_ref, q_ref, k_ref, v_ref):
    x = x_ref[...]
    h = _rms(x, lng_ref[...]).astype(BF16)

    def proj(name):
        a, b = _W1_OFF[name]
        return jnp.dot(h, w1_ref[:, a:b], preferred_element_type=F32)

    qa_ref[...] = (proj("qa") * (DH_A ** -0.5)).astype(BF16)
    ka_ref[...] = proj("ka").astype(BF16)
    va_ref[...] = proj("va").astype(BF16)
    za = proj("za")
    sza_ref[...] = (za * _sigmoid(za)).astype(BF16)
    zb = proj("zb")
    szb_ref[...] = (zb * _sigmoid(zb)).astype(BF16)
    gga_ref[...] = _sigmoid(proj("ga") + bg_ref[:, :D_MODEL]).astype(BF16)
    ggb_ref[...] = _sigmoid(proj("gb") + bg_ref[:, D_MODEL:]).astype(BF16)

    cosq, slo, shi = cos_ref[...], slo_ref[...], shi_ref[...]
    cqn = _rms(proj("cq"), qg_ref[...]).astype(BF16)
    qf = jnp.dot(cqn, wq_ref[...], preferred_element_type=F32)
    scale = (NOPE + ROPE) ** -0.5
    for hh in range(HB):
        blk = qf[:, hh * LANES:(hh + 1) * LANES]
        q_ref[hh] = (_rot(blk, cosq, slo, shi) * scale).astype(BF16)

    ckvn = _rms(proj("ckv"), kvg_ref[...]).astype(BF16)
    kf = jnp.dot(ckvn, wk_ref[...], preferred_element_type=F32)
    vf = jnp.dot(ckvn, wv_ref[...], preferred_element_type=F32)
    krr = _rot(proj("kr"), cosq, slo, shi)
    lane = lax.broadcasted_iota(jnp.int32, (1, LANES), 1)
    ones_col = (lane == DV).astype(F32)
    for hh in range(HB):
        k_ref[hh] = (kf[:, hh * LANES:(hh + 1) * LANES] + krr).astype(BF16)
        v_ref[hh] = (vf[:, hh * LANES:(hh + 1) * LANES] + ones_col).astype(BF16)


def _proj_call(x2, seq, ln_g, w1, b_gate, q_norm_g, kv_norm_g, wq, wk, wv, tables, tm):
    t = x2.shape[0]
    nt = t // tm
    per_seq = seq // tm
    row = lambda i: (i, 0)
    const = lambda i: (0, 0)
    tab = lambda i: (i % per_seq, 0)
    head = lambda i: (0, i, 0)
    bs = pl.BlockSpec
    in_specs = [bs((tm, D_MODEL), row), bs((1, D_MODEL), const), bs((D_MODEL, W1_COLS), const),
                bs((1, 2 * D_MODEL), const), bs((1, Q_LORA), const), bs((1, KV_LORA), const),
                bs((Q_LORA, HB * LANES), const), bs((KV_LORA, HB * LANES), const),
                bs((KV_LORA, HB * LANES), const),
                bs((tm, LANES), tab), bs((tm, LANES), tab), bs((tm, LANES), tab)]
    out_shape = [jax.ShapeDtypeStruct((t, HA * LANES), BF16), jax.ShapeDtypeStruct((t, HKV_A * LANES), BF16),
                 jax.ShapeDtypeStruct((t, 2 * HKV_A * LANES), BF16),
                 jax.ShapeDtypeStruct((t, WIDTH_A), BF16), jax.ShapeDtypeStruct((t, WIDTH_B), BF16),
                 jax.ShapeDtypeStruct((t, D_MODEL), BF16), jax.ShapeDtypeStruct((t, D_MODEL), BF16),
                 jax.ShapeDtypeStruct((HB, t, LANES), BF16), jax.ShapeDtypeStruct((HB, t, LANES), BF16),
                 jax.ShapeDtypeStruct((HB, t, LANES), BF16)]
    out_specs = [bs((tm, HA * LANES), row), bs((tm, HKV_A * LANES), row), bs((tm, 2 * HKV_A * LANES), row),
                 bs((tm, WIDTH_A), row), bs((tm, WIDTH_B), row), bs((tm, D_MODEL), row), bs((tm, D_MODEL), row),
                 bs((HB, tm, LANES), head), bs((HB, tm, LANES), head), bs((HB, tm, LANES), head)]
    return pl.pallas_call(
        _proj_kernel, grid=(nt,), in_specs=in_specs, out_specs=out_specs, out_shape=out_shape,
        compiler_params=pltpu.CompilerParams(dimension_semantics=("arbitrary",), vmem_limit_bytes=VMEM_LIMIT),
        name="proj",
    )(x2, ln_g, w1, b_gate, q_norm_g, kv_norm_g, wq, wk, wv, *tables)


def _window_kernel(nb, relb_ref, sink_ref, bucket_ref, q_ref, kp_ref, kc_ref, kn_ref,
                   vp_ref, vc_ref, vn_ref, sza_ref, o_ref, bias_ref):
    first = (pl.program_id(0) == 0) & (pl.program_id(1) == 0)

    @pl.when(first)
    def _():
        bucket = bucket_ref[...]
        for hh in range(HA):
            acc = jnp.where(bucket < 0, NEG, 0.0).astype(F32)
            for b in range(NUM_BUCKETS):
                acc = acc + jnp.where(bucket == b, relb_ref[b, hh], 0.0)
            bias_ref[hh] = acc

    n = pl.program_id(1)
    col = lax.broadcasted_iota(jnp.int32, (1, 3 * BLK), 1)
    gone = ((col < BLK) & (n == 0)) | ((col >= 2 * BLK) & (n == nb - 1))
    pen = jnp.where(gone, NEG, 0.0).astype(F32)

    outs = []
    for g in range(HKV_A):
        ksl = slice(g * LANES, (g + 1) * LANES)
        kw = jnp.concatenate([kp_ref[:, ksl], kc_ref[:, ksl], kn_ref[:, ksl]], axis=0)
        for pair in range(G_A // 2):
            acc = None
            for odd in range(2):
                hh = g * G_A + pair * 2 + odd
                q = q_ref[:, hh * LANES:(hh + 1) * LANES]
                s = lax.dot_general(q, kw, (((1,), (1,)), ((), ())), preferred_element_type=F32)
                s = s + bias_ref[hh] + pen
                sk = sink_ref[hh]
                m = jnp.maximum(jnp.max(s, axis=-1, keepdims=True), sk)
                p = jnp.exp(s - m)
                denom = jnp.sum(p, axis=-1, keepdims=True) + jnp.exp(sk - m)
                pn = (p / denom).astype(BF16)
                vb = odd * HKV_A + g
                vsl = slice(vb * LANES, (vb + 1) * LANES)
                vw = jnp.concatenate([vp_ref[:, vsl], vc_ref[:, vsl], vn_ref[:, vsl]], axis=0)
                o = jnp.dot(pn, vw, preferred_element_type=F32)
                acc = o if acc is None else acc + o
            outs.append(acc)
    y = jnp.concatenate(outs, axis=1)
    o_ref[...] = (y * sza_ref[...].astype(F32)).astype(BF16)


def _window_call(qa, ka, va, sza, rel_bias, sink, bucket, batch, seq):
    t = qa.shape[0]
    nb = seq // BLK
    bs = pl.BlockSpec
    cur = lambda b, n: (b * nb + n, 0)
    prev = lambda b, n: (b * nb + jnp.maximum(n - 1, 0), 0)
    nxt = lambda b, n: (b * nb + jnp.minimum(n + 1, nb - 1), 0)
    smem = functools.partial(bs, memory_space=pltpu.SMEM)
    kw, vw = HKV_A * LANES, 2 * HKV_A * LANES
    in_specs = [smem(), smem(), bs((BLK, 3 * BLK), lambda b, n: (0, 0)),
                bs((BLK, HA * LANES), cur),
                bs((BLK, kw), prev), bs((BLK, kw), cur), bs((BLK, kw), nxt),
                bs((BLK, vw), prev), bs((BLK, vw), cur), bs((BLK, vw), nxt),
                bs((BLK, WIDTH_A), cur)]
    return pl.pallas_call(
        functools.partial(_window_kernel, nb),
        grid=(batch, nb), in_specs=in_specs, out_specs=bs((BLK, WIDTH_A), cur),
        out_shape=jax.ShapeDtypeStruct((t, WIDTH_A), BF16),
        scratch_shapes=[pltpu.VMEM((HA, BLK, 3 * BLK), F32)],
        compiler_params=pltpu.CompilerParams(dimension_semantics=("arbitrary", "arbitrary"),
                                             vmem_limit_bytes=VMEM_LIMIT),
        name="window",
    )(rel_bias, sink, bucket, qa, ka, ka, ka, va, va, va, sza)


def _mla_kernel(seq, bk, q_ref, k_ref, v_ref, szb_ref, o_ref):
    bq = q_ref.shape[1]
    outs = []
    for hh in range(2):
        q = q_ref[hh]

        def body(j, carry, hh=hh, q=q):
            m, acc = carry
            start = pl.multiple_of(j * bk, bk)
            k = k_ref[hh, pl.ds(start, bk), :]
            v = v_ref[hh, pl.ds(start, bk), :]
            s = lax.dot_general(q, k, (((1,), (1,)), ((), ())), preferred_element_type=F32)
            m_new = jnp.maximum(m, jnp.max(s, axis=-1, keepdims=True))
            alpha = jnp.exp(m - m_new)
            p = jnp.exp(s - m_new).astype(BF16)
            acc = acc * alpha + jnp.dot(p, v, preferred_element_type=F32)
            return m_new, acc

        m0 = jnp.full((bq, 1), NEG, F32)
        acc0 = jnp.zeros((bq, LANES), F32)
        _, acc = lax.fori_loop(0, seq // bk, body, (m0, acc0))
        outs.append(acc[:, :DV] / acc[:, DV:DV + 1])
    y = jnp.concatenate(outs, axis=1)
    o_ref[...] = (y * szb_ref[...].astype(F32)).astype(BF16)


def _mla_call(q, k, v, szb, batch, seq, bq, bk):
    t = q.shape[1]
    nq = seq // bq
    bs = pl.BlockSpec
    return pl.pallas_call(
        functools.partial(_mla_kernel, seq, bk),
        grid=(batch, HB // 2, nq),
        in_specs=[bs((2, bq, LANES), lambda b, hp, i: (hp, b * nq + i, 0)),
                  bs((2, seq, LANES), lambda b, hp, i: (hp, b, 0)),
                  bs((2, seq, LANES), lambda b, hp, i: (hp, b, 0)),
                  bs((bq, LANES), lambda b, hp, i: (b * nq + i, hp))],
        out_specs=bs((bq, LANES), lambda b, hp, i: (b * nq + i, hp)),
        out_shape=jax.ShapeDtypeStruct((t, WIDTH_B), BF16),
        compiler_params=pltpu.CompilerParams(dimension_semantics=("arbitrary", "arbitrary", "arbitrary"),
                                             vmem_limit_bytes=VMEM_LIMIT),
        name="mla",
    )(q, k, v, szb)


def _merge_kernel(x_ref, a_ref, b_ref, gga_ref, ggb_ref, wpa_ref, wpb_ref, wo_ref, fg_ref, o_ref):
    oa = jnp.dot(a_ref[...], wpa_ref[...], preferred_element_type=F32)
    ob = jnp.dot(b_ref[...], wpb_ref[...], preferred_element_type=F32)
    merged = gga_ref[...].astype(F32) * oa + ggb_ref[...].astype(F32) * ob
    y = x_ref[...] + jnp.dot(merged.astype(BF16), wo_ref[...], preferred_element_type=F32)
    o_ref[...] = _rms(y, fg_ref[...])


def _merge_call(x2, a, b, gga, ggb, wpa, wpb, wo, fg, tm):
    t = x2.shape[0]
    bs = pl.BlockSpec
    row = lambda i: (i, 0)
    const = lambda i: (0, 0)
    return pl.pallas_call(
        _merge_kernel, grid=(t // tm,),
        in_specs=[bs((tm, D_MODEL), row), bs((tm, WIDTH_A), row), bs((tm, WIDTH_B), row),
                  bs((tm, D_MODEL), row), bs((tm, D_MODEL), row),
                  bs((WIDTH_A, D_MODEL), const), bs((WIDTH_B, D_MODEL), const), bs((D_MODEL, D_MODEL), const),
                  bs((1, D_MODEL), const)],
        out_specs=bs((tm, D_MODEL), row),
        out_shape=jax.ShapeDtypeStruct((t, D_MODEL), F32),
        compiler_params=pltpu.CompilerParams(dimension_semantics=("arbitrary",), vmem_limit_bytes=VMEM_LIMIT),
        name="merge",
    )(x2, a, b, gga, ggb, wpa, wpb, wo, fg)


def _tiles(seq):
    tm = min(256, seq)
    bq = min(256, seq)
    bk = min(512, seq)
    return tm, bq, bk


def _trunk(x, p):
    batch, seq, _ = x.shape
    tm, bq, bk = _tiles(seq)
    x2 = x.reshape(batch * seq, D_MODEL)
    tables = _rope_tables(seq)
    (qa, ka, va, sza, szb, gga, ggb, q, k, v) = _proj_call(
        x2, seq, p["ln_g"], p["w1"], p["b_gate"], p["q_norm_g"], p["kv_norm_g"], p["wq"], p["wk"], p["wv"],
        tables, tm)
    a = _window_call(qa, ka, va, sza, p["rel_bias"], p["sink"], p["bucket"], batch, seq)
    b = _mla_call(q, k, v, szb, batch, seq, bq, bk)
    y = _merge_call(x2, a, b, gga, ggb, p["wpa"], p["wpb"], p["wo"], p["final_g"], tm)
    return y.reshape(batch, seq, D_MODEL)


def kernel(x_prompt, x_sample, ln_g, w_in, b_gate, sink_a, q_norm_g, kv_norm_g, w_uq, w_ukv,
           w_proj_a, w_proj_b, w_out, rel_bias, final_g):
    assert ln_g.shape[0] == 1, "one layer"
    w1, wq, wk, wv = _pack_weights(w_in[0], w_uq[0], w_ukv[0])
    p = dict(ln_g=ln_g[0][None, :], w1=w1, b_gate=b_gate[0][None, :], q_norm_g=q_norm_g[0][None, :],
             kv_norm_g=kv_norm_g[0][None, :], wq=wq, wk=wk, wv=wv,
             rel_bias=rel_bias.astype(F32), sink=sink_a[0].astype(F32), bucket=_bucket_table(),
             wpa=w_proj_a[0].astype(BF16), wpb=w_proj_b[0].astype(BF16), wo=w_out[0].astype(BF16),
             final_g=final_g[None, :])
    return (_trunk(x_prompt, p), _trunk(x_sample, p))
```

```python
import functools
import math

import jax
import jax.numpy as jnp
from jax import lax
from jax.experimental import pallas as pl
from jax.experimental.pallas import tpu as pltpu

D_MODEL = 1024
BLK = 128
WINDOW = 128
HA = 8
HKV_A = 2
G_A = HA // HKV_A
DH_A = 64
WIDTH_A = HA * DH_A
NUM_BUCKETS = 32
MAX_DISTANCE = 128
HB = 8
Q_LORA = 256
KV_LORA = 128
NOPE = 64
ROPE = 32
DV = 64
WIDTH_B = HB * DV
ROPE_THETA = 10000.0
EPS = 1e-6
NEG = -1e30
LOG2E = math.log2(math.e)
MLA_ROWS = 32

LANES = 128
VMEM_LIMIT = 56 * 1024 * 1024

F32 = jnp.float32
BF16 = jnp.bfloat16

_W1_GROUPS = (("qa", HA * LANES), ("ka", HKV_A * LANES), ("va", 2 * HKV_A * LANES), ("za", WIDTH_A),
              ("cq", Q_LORA), ("ckv", KV_LORA), ("kr", LANES), ("zb", WIDTH_B),
              ("ga", D_MODEL), ("gb", D_MODEL))
_W1_OFF = {}
_o = 0
for _n, _w in _W1_GROUPS:
    _W1_OFF[_n] = (_o, _o + _w)
    _o += _w
W1_COLS = _o


def _pad_heads(w, n_heads, width, lane_off=0):
    k = w.shape[0]
    w = w.reshape(k, n_heads, width)
    out = jnp.zeros((k, n_heads, LANES), w.dtype)
    out = out.at[:, :, lane_off:lane_off + width].set(w)
    return out.reshape(k, n_heads * LANES)


def _pack_weights(w_in, w_uq, w_ukv):
    idx = [0]
    for s in (WIDTH_A, HKV_A * DH_A, HKV_A * DH_A, WIDTH_A, Q_LORA, KV_LORA, ROPE, WIDTH_B, D_MODEL, D_MODEL):
        idx.append(idx[-1] + s)
    qa, ka, va, za, cq, ckv, kr, zb, ga, gb = [w_in[:, idx[i]:idx[i + 1]] for i in range(10)]
    va_lo = _pad_heads(va, HKV_A, DH_A, 0)
    va_hi = _pad_heads(va, HKV_A, DH_A, DH_A)
    kr_p = jnp.zeros((D_MODEL, LANES), w_in.dtype).at[:, NOPE:NOPE + ROPE].set(kr)
    w1 = jnp.concatenate([_pad_heads(qa, HA, DH_A), _pad_heads(ka, HKV_A, DH_A), va_lo, va_hi, za,
                          cq, ckv, kr_p, zb, ga, gb], axis=1).astype(BF16)
    wq = _pad_heads(w_uq, HB, NOPE + ROPE).astype(BF16)
    kvf = w_ukv.reshape(KV_LORA, HB, NOPE + DV)
    wk = _pad_heads(kvf[:, :, :NOPE].reshape(KV_LORA, HB * NOPE), HB, NOPE).astype(BF16)
    wv = _pad_heads(kvf[:, :, NOPE:].reshape(KV_LORA, HB * DV), HB, DV).astype(BF16)
    return w1, wq, wk, wv


def _rope_tables(seq):
    half = ROPE // 2
    inv = jnp.power(ROPE_THETA, -jnp.arange(half, dtype=F32) / half)
    ang = jnp.arange(seq, dtype=F32)[:, None] * inv[None, :]
    cos, sin = jnp.cos(ang), jnp.sin(ang)
    z = jnp.zeros((seq, half), F32)
    pad = jnp.zeros((seq, LANES - NOPE - ROPE), F32)
    ones = jnp.ones((seq, NOPE), F32)
    zn = jnp.zeros((seq, NOPE), F32)
    cosq = jnp.concatenate([ones, cos, cos, pad], axis=1)
    sin_lo = jnp.concatenate([zn, -sin, z, pad], axis=1)
    sin_hi = jnp.concatenate([zn, z, sin, pad], axis=1)
    return cosq, sin_lo, sin_hi


def _bucket_table():
    qi = jnp.arange(BLK, dtype=jnp.int32)[:, None]
    kj = jnp.arange(3 * BLK, dtype=jnp.int32)[None, :]
    rel = kj - BLK - qi
    nb = NUM_BUCKETS // 2
    max_exact = nb // 2
    ret = (rel > 0).astype(jnp.int32) * nb
    n = jnp.abs(rel)
    nf = jnp.maximum(n, 1).astype(F32)
    large = max_exact + (jnp.log(nf / max_exact) / math.log(MAX_DISTANCE / max_exact)
                         * (nb - max_exact)).astype(jnp.int32)
    large = jnp.minimum(large, nb - 1)
    bucket = ret + jnp.where(n < max_exact, n, large)
    return jnp.where(n <= WINDOW, bucket, -1)


def _rms(v, g):
    return v * lax.rsqrt(jnp.mean(v * v, axis=-1, keepdims=True) + EPS) * g


def _sigmoid(v):
    return 1.0 / (1.0 + jnp.exp(-v))


def _rot(v, cosq, sin_lo, sin_hi):
    return v * cosq + pltpu.roll(v, LANES - ROPE // 2, 1) * sin_lo + pltpu.roll(v, ROPE // 2, 1) * sin_hi


def _proj_kernel(x_ref, lng_ref, w1_ref, bg_ref, qg_ref, kvg_ref, wq_ref, wk_ref, wv_ref,
                 cos_ref, slo_ref, shi_ref,
                 qa_ref, ka_ref, va_ref, sza_ref, szb_ref, gga_ref, ggb_ref, q_ref, k_ref, v_ref):
    x = x_ref[...]
    h = _rms(x, lng_ref[...]).astype(BF16)

    def proj(name):
        a, b = _W1_OFF[name]
        return jnp.dot(h, w1_ref[:, a:b], preferred_element_type=F32)

    qa_ref[...] = (proj("qa") * (DH_A ** -0.5)).astype(BF16)
    ka_ref[...] = proj("ka").astype(BF16)
    va_ref[...] = proj("va").astype(BF16)
    za = proj("za")
    sza_ref[...] = (za * _sigmoid(za)).astype(BF16)
    zb = proj("zb")
    szb_ref[...] = (zb * _sigmoid(zb)).astype(BF16)
    gga_ref[...] = _sigmoid(proj("ga") + bg_ref[:, :D_MODEL]).astype(BF16)
    ggb_ref[...] = _sigmoid(proj("gb") + bg_ref[:, D_MODEL:]).astype(BF16)

    cosq, slo, shi = cos_ref[...], slo_ref[...], shi_ref[...]
    cqn = _rms(proj("cq"), qg_ref[...]).astype(BF16)
    qf = jnp.dot(cqn, wq_ref[...], preferred_element_type=F32)
    scale = (NOPE + ROPE) ** -0.5 * LOG2E
    for hh in range(HB):
        blk = qf[:, hh * LANES:(hh + 1) * LANES]
        q_ref[hh] = (_rot(blk, cosq, slo, shi) * scale).astype(BF16)

    ckvn = _rms(proj("ckv"), kvg_ref[...]).astype(BF16)
    kf = jnp.dot(ckvn, wk_ref[...], preferred_element_type=F32)
    vf = jnp.dot(ckvn, wv_ref[...], preferred_element_type=F32)
    krr = _rot(proj("kr"), cosq, slo, shi)
    lane = lax.broadcasted_iota(jnp.int32, (1, LANES), 1)
    ones_col = (lane == DV).astype(F32)
    for hh in range(HB):
        k_ref[hh] = (kf[:, hh * LANES:(hh + 1) * LANES] + krr).T.astype(BF16)
        v_ref[hh] = (vf[:, hh * LANES:(hh + 1) * LANES] + ones_col).astype(BF16)


def _proj_call(x2, seq, ln_g, w1, b_gate, q_norm_g, kv_norm_g, wq, wk, wv, tables, tm):
    t = x2.shape[0]
    nt = t // tm
    per_seq = seq // tm
    row = lambda i: (i, 0)
    const = lambda i: (0, 0)
    tab = lambda i: (i % per_seq, 0)
    head = lambda i: (0, i, 0)
    bs = pl.BlockSpec
    in_specs = [bs((tm, D_MODEL), row), bs((1, D_MODEL), const), bs((D_MODEL, W1_COLS), const),
                bs((1, 2 * D_MODEL), const), bs((1, Q_LORA), const), bs((1, KV_LORA), const),
                bs((Q_LORA, HB * LANES), const), bs((KV_LORA, HB * LANES), const),
                bs((KV_LORA, HB * LANES), const),
                bs((tm, LANES), tab), bs((tm, LANES), tab), bs((tm, LANES), tab)]
    out_shape = [jax.ShapeDtypeStruct((t, HA * LANES), BF16), jax.ShapeDtypeStruct((t, HKV_A * LANES), BF16),
                 jax.ShapeDtypeStruct((t, 2 * HKV_A * LANES), BF16),
                 jax.ShapeDtypeStruct((t, WIDTH_A), BF16), jax.ShapeDtypeStruct((t, WIDTH_B), BF16),
                 jax.ShapeDtypeStruct((t, D_MODEL), BF16), jax.ShapeDtypeStruct((t, D_MODEL), BF16),
                 jax.ShapeDtypeStruct((HB, t, LANES), BF16), jax.ShapeDtypeStruct((HB, LANES, t), BF16),
                 jax.ShapeDtypeStruct((HB, t, LANES), BF16)]
    out_specs = [bs((tm, HA * LANES), row), bs((tm, HKV_A * LANES), row), bs((tm, 2 * HKV_A * LANES), row),
                 bs((tm, WIDTH_A), row), bs((tm, WIDTH_B), row), bs((tm, D_MODEL), row), bs((tm, D_MODEL), row),
                 bs((HB, tm, LANES), head), bs((HB, LANES, tm), lambda i: (0, 0, i)), bs((HB, tm, LANES), head)]
    return pl.pallas_call(
        _proj_kernel, grid=(nt,), in_specs=in_specs, out_specs=out_specs, out_shape=out_shape,
        compiler_params=pltpu.CompilerParams(dimension_semantics=("arbitrary",), vmem_limit_bytes=VMEM_LIMIT),
        name="proj",
    )(x2, ln_g, w1, b_gate, q_norm_g, kv_norm_g, wq, wk, wv, *tables)


def _window_kernel(nb, relb_ref, sink_ref, bucket_ref, q_ref, kp_ref, kc_ref, kn_ref,
                   vp_ref, vc_ref, vn_ref, sza_ref, o_ref, bias_ref):
    first = (pl.program_id(0) == 0) & (pl.program_id(1) == 0)

    @pl.when(first)
    def _():
        bucket = bucket_ref[...]
        for hh in range(HA):
            acc = jnp.where(bucket < 0, NEG, 0.0).astype(F32)
            for b in range(NUM_BUCKETS):
                acc = acc + jnp.where(bucket == b, relb_ref[b, hh], 0.0)
            bias_ref[hh] = acc

    n = pl.program_id(1)
    col = lax.broadcasted_iota(jnp.int32, (1, 3 * BLK), 1)
    gone = ((col < BLK) & (n == 0)) | ((col >= 2 * BLK) & (n == nb - 1))
    pen = jnp.where(gone, NEG, 0.0).astype(F32)

    outs = []
    for g in range(HKV_A):
        ksl = slice(g * LANES, (g + 1) * LANES)
        kw = jnp.concatenate([kp_ref[:, ksl], kc_ref[:, ksl], kn_ref[:, ksl]], axis=0)
        for pair in range(G_A // 2):
            acc = None
            for odd in range(2):
                hh = g * G_A + pair * 2 + odd
                q = q_ref[:, hh * LANES:(hh + 1) * LANES]
                s = lax.dot_general(q, kw, (((1,), (1,)), ((), ())), preferred_element_type=F32)
                s = s + bias_ref[hh] + pen
                sk = sink_ref[hh]
                m = jnp.maximum(jnp.max(s, axis=-1, keepdims=True), sk)
                p = jnp.exp(s - m)
                denom = jnp.sum(p, axis=-1, keepdims=True) + jnp.exp(sk - m)
                pn = (p / denom).astype(BF16)
                vb = odd * HKV_A + g
                vsl = slice(vb * LANES, (vb + 1) * LANES)
                vw = jnp.concatenate([vp_ref[:, vsl], vc_ref[:, vsl], vn_ref[:, vsl]], axis=0)
                o = jnp.dot(pn, vw, preferred_element_type=F32)
                acc = o if acc is None else acc + o
            outs.append(acc)
    y = jnp.concatenate(outs, axis=1)
    o_ref[...] = (y * sza_ref[...].astype(F32)).astype(BF16)


def _window_call(qa, ka, va, sza, rel_bias, sink, bucket, batch, seq):
    t = qa.shape[0]
    nb = seq // BLK
    bs = pl.BlockSpec
    cur = lambda b, n: (b * nb + n, 0)
    prev = lambda b, n: (b * nb + jnp.maximum(n - 1, 0), 0)
    nxt = lambda b, n: (b * nb + jnp.minimum(n + 1, nb - 1), 0)
    smem = functools.partial(bs, memory_space=pltpu.SMEM)
    kw, vw = HKV_A * LANES, 2 * HKV_A * LANES
    in_specs = [smem(), smem(), bs((BLK, 3 * BLK), lambda b, n: (0, 0)),
                bs((BLK, HA * LANES), cur),
                bs((BLK, kw), prev), bs((BLK, kw), cur), bs((BLK, kw), nxt),
                bs((BLK, vw), prev), bs((BLK, vw), cur), bs((BLK, vw), nxt),
                bs((BLK, WIDTH_A), cur)]
    return pl.pallas_call(
        functools.partial(_window_kernel, nb),
        grid=(batch, nb), in_specs=in_specs, out_specs=bs((BLK, WIDTH_A), cur),
        out_shape=jax.ShapeDtypeStruct((t, WIDTH_A), BF16),
        scratch_shapes=[pltpu.VMEM((HA, BLK, 3 * BLK), F32)],
        compiler_params=pltpu.CompilerParams(dimension_semantics=("arbitrary", "arbitrary"),
                                             vmem_limit_bytes=VMEM_LIMIT),
        name="window",
    )(rel_bias, sink, bucket, qa, ka, ka, ka, va, va, va, sza)


def _mla_kernel(seq, bk, q_ref, kt_ref, v_ref, szb_ref, o_ref, s_all, p_all, m_all, a_all, acc_all):
    bq = q_ref.shape[1]
    nkv = seq // bk
    for hh in range(2):
        q = q_ref[hh]
        s_scr, p_scr, m_scr, a_scr, acc_scr = (r.at[hh] for r in (s_all, p_all, m_all, a_all, acc_all))

        def scores(j, slot, hh=hh, q=q):
            start = pl.multiple_of(j * bk, bk)
            s_scr[slot] = jnp.dot(q, kt_ref[hh, :, pl.ds(start, bk)], preferred_element_type=F32)

        def values(j, slot, hh=hh):
            start = pl.multiple_of(j * bk, bk)
            pv = jnp.dot(p_scr[slot], v_ref[hh, pl.ds(start, bk), :], preferred_element_type=F32)
            acc_scr[...] = acc_scr[...] * a_scr[...] + pv

        def softmax(slot):
            for r in range(0, bq, MLA_ROWS):
                rows = slice(r, r + MLA_ROWS)
                s = s_scr[slot, rows, :]
                m_old = m_scr[rows, :]
                m_new = jnp.maximum(m_old, jnp.max(s, axis=-1, keepdims=True))
                a_scr[rows, :] = jnp.exp2(m_old - m_new)
                m_scr[rows, :] = m_new
                p_scr[slot, rows, :] = jnp.exp2(s - m_new).astype(BF16)

        scores(0, 0)
        m_scr[...] = jnp.full(m_scr.shape, NEG, F32)
        a_scr[...] = jnp.ones(a_scr.shape, F32)
        acc_scr[...] = jnp.zeros(acc_scr.shape, F32)
        p_scr[1] = jnp.zeros(p_scr.shape[1:], BF16)

        def pair(t, carry, scores=scores, values=values, softmax=softmax):
            j = 2 * t
            scores(j + 1, 1)
            values(jnp.maximum(j - 1, 0), 1)
            softmax(0)
            scores(jnp.minimum(j + 2, nkv - 1), 0)
            values(j, 0)
            softmax(1)
            return carry

        lax.fori_loop(0, nkv // 2, pair, 0)
        values(nkv - 1, 1)
        acc = acc_scr[...]
        y = acc[:, :DV] / acc[:, DV:DV + 1]
        cols = slice(hh * DV, (hh + 1) * DV)
        o_ref[:, cols] = (y * szb_ref[:, cols].astype(F32)).astype(BF16)


def _mla_call(q, k, v, szb, batch, seq, bq, bk):
    t = q.shape[1]
    nq = seq // bq
    assert seq % (2 * bk) == 0 and bq % MLA_ROWS == 0
    bs = pl.BlockSpec
    return pl.pallas_call(
        functools.partial(_mla_kernel, seq, bk),
        grid=(batch, HB // 2, nq),
        in_specs=[bs((2, bq, LANES), lambda b, hp, i: (hp, b * nq + i, 0)),
                  bs((2, LANES, seq), lambda b, hp, i: (hp, 0, b)),
                  bs((2, seq, LANES), lambda b, hp, i: (hp, b, 0)),
                  bs((bq, LANES), lambda b, hp, i: (b * nq + i, hp))],
        out_specs=bs((bq, LANES), lambda b, hp, i: (b * nq + i, hp)),
        out_shape=jax.ShapeDtypeStruct((t, WIDTH_B), BF16),
        scratch_shapes=[pltpu.VMEM((2, 2, bq, bk), F32), pltpu.VMEM((2, 2, bq, bk), BF16),
                        pltpu.VMEM((2, bq, 1), F32), pltpu.VMEM((2, bq, 1), F32),
                        pltpu.VMEM((2, bq, LANES), F32)],
        compiler_params=pltpu.CompilerParams(dimension_semantics=("arbitrary", "arbitrary", "arbitrary"),
                                             vmem_limit_bytes=VMEM_LIMIT),
        name="mla",
    )(q, k, v, szb)


def _merge_kernel(x_ref, a_ref, b_ref, gga_ref, ggb_ref, wpa_ref, wpb_ref, wo_ref, fg_ref, o_ref):
    oa = jnp.dot(a_ref[...], wpa_ref[...], preferred_element_type=F32)
    ob = jnp.dot(b_ref[...], wpb_ref[...], preferred_element_type=F32)
    merged = gga_ref[...].astype(F32) * oa + ggb_ref[...].astype(F32) * ob
    y = x_ref[...] + jnp.dot(merged.astype(BF16), wo_ref[...], preferred_element_type=F32)
    o_ref[...] = _rms(y, fg_ref[...])


def _merge_call(x2, a, b, gga, ggb, wpa, wpb, wo, fg, tm):
    t = x2.shape[0]
    bs = pl.BlockSpec
    row = lambda i: (i, 0)
    const = lambda i: (0, 0)
    return pl.pallas_call(
        _merge_kernel, grid=(t // tm,),
        in_specs=[bs((tm, D_MODEL), row), bs((tm, WIDTH_A), row), bs((tm, WIDTH_B), row),
                  bs((tm, D_MODEL), row), bs((tm, D_MODEL), row),
                  bs((WIDTH_A, D_MODEL), const), bs((WIDTH_B, D_MODEL), const), bs((D_MODEL, D_MODEL), const),
                  bs((1, D_MODEL), const)],
        out_specs=bs((tm, D_MODEL), row),
        out_shape=jax.ShapeDtypeStruct((t, D_MODEL), F32),
        compiler_params=pltpu.CompilerParams(dimension_semantics=("arbitrary",), vmem_limit_bytes=VMEM_LIMIT),
        name="merge",
    )(x2, a, b, gga, ggb, wpa, wpb, wo, fg)


def _tiles(seq):
    tm = min(256, seq)
    bq = min(512, seq)
    bk = min(512, seq // 2)
    return tm, bq, bk


def _trunk(x, p):
    batch, seq, _ = x.shape
    tm, bq, bk = _tiles(seq)
    x2 = x.reshape(batch * seq, D_MODEL)
    tables = _rope_tables(seq)
    (qa, ka, va, sza, szb, gga, ggb, q, k, v) = _proj_call(
        x2, seq, p["ln_g"], p["w1"], p["b_gate"], p["q_norm_g"], p["kv_norm_g"], p["wq"], p["wk"], p["wv"],
        tables, tm)
    a = _window_call(qa, ka, va, sza, p["rel_bias"], p["sink"], p["bucket"], batch, seq)
    b = _mla_call(q, k, v, szb, batch, seq, bq, bk)
    y = _merge_call(x2, a, b, gga, ggb, p["wpa"], p["wpb"], p["wo"], p["final_g"], tm)
    return y.reshape(batch, seq, D_MODEL)


def kernel(x_prompt, x_sample, ln_g, w_in, b_gate, sink_a, q_norm_g, kv_norm_g, w_uq, w_ukv,
           w_proj_a, w_proj_b, w_out, rel_bias, final_g):
    assert ln_g.shape[0] == 1, "one layer"
    w1, wq, wk, wv = _pack_weights(w_in[0], w_uq[0], w_ukv[0])
    p = dict(ln_g=ln_g[0][None, :], w1=w1, b_gate=b_gate[0][None, :], q_norm_g=q_norm_g[0][None, :],
             kv_norm_g=kv_norm_g[0][None, :], wq=wq, wk=wk, wv=wv,
             rel_bias=rel_bias.astype(F32), sink=sink_a[0].astype(F32), bucket=_bucket_table(),
             wpa=w_proj_a[0].astype(BF16), wpb=w_proj_b[0].astype(BF16), wo=w_out[0].astype(BF16),
             final_g=final_g[None, :])
    return (_trunk(x_prompt, p), _trunk(x_sample, p))
```

```python
import functools
import math

import jax
import jax.numpy as jnp
from jax import lax
from jax.experimental import pallas as pl
from jax.experimental.pallas import tpu as pltpu

D_MODEL = 1024
BLK = 128
WINDOW = 128
HA = 8
HKV_A = 2
G_A = HA // HKV_A
DH_A = 64
WIDTH_A = HA * DH_A
NUM_BUCKETS = 32
MAX_DISTANCE = 128
HB = 8
Q_LORA = 256
KV_LORA = 128
NOPE = 64
ROPE = 32
DV = 64
WIDTH_B = HB * DV
ROPE_THETA = 10000.0
EPS = 1e-6
NEG = -1e30
LOG2E = math.log2(math.e)
MLA_UNROLL = 4
MLA_ROWS = 16

LANES = 128
VMEM_LIMIT = 56 * 1024 * 1024

F32 = jnp.float32
BF16 = jnp.bfloat16

_W1_GROUPS = (("qa", HA * LANES), ("ka", HKV_A * LANES), ("va", 2 * HKV_A * LANES), ("za", WIDTH_A),
              ("cq", Q_LORA), ("ckv", KV_LORA), ("kr", LANES), ("zb", WIDTH_B),
              ("ga", D_MODEL), ("gb", D_MODEL))
_W1_OFF = {}
_o = 0
for _n, _w in _W1_GROUPS:
    _W1_OFF[_n] = (_o, _o + _w)
    _o += _w
W1_COLS = _o


def _pad_heads(w, n_heads, width, lane_off=0):
    k = w.shape[0]
    w = w.reshape(k, n_heads, width)
    out = jnp.zeros((k, n_heads, LANES), w.dtype)
    out = out.at[:, :, lane_off:lane_off + width].set(w)
    return out.reshape(k, n_heads * LANES)


def _pack_weights(w_in, w_uq, w_ukv):
    idx = [0]
    for s in (WIDTH_A, HKV_A * DH_A, HKV_A * DH_A, WIDTH_A, Q_LORA, KV_LORA, ROPE, WIDTH_B, D_MODEL, D_MODEL):
        idx.append(idx[-1] + s)
    qa, ka, va, za, cq, ckv, kr, zb, ga, gb = [w_in[:, idx[i]:idx[i + 1]] for i in range(10)]
    va_lo = _pad_heads(va, HKV_A, DH_A, 0)
    va_hi = _pad_heads(va, HKV_A, DH_A, DH_A)
    kr_p = jnp.zeros((D_MODEL, LANES), w_in.dtype).at[:, NOPE:NOPE + ROPE].set(kr)
    w1 = jnp.concatenate([_pad_heads(qa, HA, DH_A), _pad_heads(ka, HKV_A, DH_A), va_lo, va_hi, za,
                          cq, ckv, kr_p, zb, ga, gb], axis=1).astype(BF16)
    wq = _pad_heads(w_uq, HB, NOPE + ROPE).astype(BF16)
    kvf = w_ukv.reshape(KV_LORA, HB, NOPE + DV)
    wk = _pad_heads(kvf[:, :, :NOPE].reshape(KV_LORA, HB * NOPE), HB, NOPE).astype(BF16)
    wv = _pad_heads(kvf[:, :, NOPE:].reshape(KV_LORA, HB * DV), HB, DV).astype(BF16)
    return w1, wq, wk, wv


def _rope_tables(seq):
    half = ROPE // 2
    inv = jnp.power(ROPE_THETA, -jnp.arange(half, dtype=F32) / half)
    ang = jnp.arange(seq, dtype=F32)[:, None] * inv[None, :]
    cos, sin = jnp.cos(ang), jnp.sin(ang)
    z = jnp.zeros((seq, half), F32)
    pad = jnp.zeros((seq, LANES - NOPE - ROPE), F32)
    ones = jnp.ones((seq, NOPE), F32)
    zn = jnp.zeros((seq, NOPE), F32)
    cosq = jnp.concatenate([ones, cos, cos, pad], axis=1)
    sin_lo = jnp.concatenate([zn, -sin, z, pad], axis=1)
    sin_hi = jnp.concatenate([zn, z, sin, pad], axis=1)
    return cosq, sin_lo, sin_hi


def _bucket_table():
    qi = jnp.arange(BLK, dtype=jnp.int32)[:, None]
    kj = jnp.arange(3 * BLK, dtype=jnp.int32)[None, :]
    rel = kj - BLK - qi
    nb = NUM_BUCKETS // 2
    max_exact = nb // 2
    ret = (rel > 0).astype(jnp.int32) * nb
    n = jnp.abs(rel)
    nf = jnp.maximum(n, 1).astype(F32)
    large = max_exact + (jnp.log(nf / max_exact) / math.log(MAX_DISTANCE / max_exact)
                         * (nb - max_exact)).astype(jnp.int32)
    large = jnp.minimum(large, nb - 1)
    bucket = ret + jnp.where(n < max_exact, n, large)
    return jnp.where(n <= WINDOW, bucket, -1)


def _rms(v, g):
    return v * lax.rsqrt(jnp.mean(v * v, axis=-1, keepdims=True) + EPS) * g


def _sigmoid(v):
    return 1.0 / (1.0 + jnp.exp(-v))


def _rot(v, cosq, sin_lo, sin_hi):
    return v * cosq + pltpu.roll(v, LANES - ROPE // 2, 1) * sin_lo + pltpu.roll(v, ROPE // 2, 1) * sin_hi


def _proj_kernel(x_ref, lng_ref, w1_ref, bg_ref, qg_ref, kvg_ref, wq_ref, wk_ref, wv_ref,
                 cos_ref, slo_ref, shi_ref,
                 qa_ref, ka_ref, va_ref, sza_ref, szb_ref, gga_ref, ggb_ref, q_ref, k_ref, v_ref):
    x = x_ref[...]
    h = _rms(x, lng_ref[...]).astype(BF16)

    def proj(name):
        a, b = _W1_OFF[name]
        return jnp.dot(h, w1_ref[:, a:b], preferred_element_type=F32)

    qa_ref[...] = (proj("qa") * (DH_A ** -0.5)).astype(BF16)
    ka_ref[...] = proj("ka").astype(BF16)
    va_ref[...] = proj("va").astype(BF16)
    za = proj("za")
    sza_ref[...] = (za * _sigmoid(za)).astype(BF16)
    zb = proj("zb")
    szb_ref[...] = (zb * _sigmoid(zb)).astype(BF16)
    gga_ref[...] = _sigmoid(proj("ga") + bg_ref[:, :D_MODEL]).astype(BF16)
    ggb_ref[...] = _sigmoid(proj("gb") + bg_ref[:, D_MODEL:]).astype(BF16)

    cosq, slo, shi = cos_ref[...], slo_ref[...], shi_ref[...]
    cqn = _rms(proj("cq"), qg_ref[...]).astype(BF16)
    qf = jnp.dot(cqn, wq_ref[...], preferred_element_type=F32)
    scale = (NOPE + ROPE) ** -0.5 * LOG2E
    for hh in range(HB):
        blk = qf[:, hh * LANES:(hh + 1) * LANES]
        q_ref[hh] = (_rot(blk, cosq, slo, shi) * scale).T.astype(BF16)

    ckvn = _rms(proj("ckv"), kvg_ref[...]).astype(BF16)
    kf = jnp.dot(ckvn, wk_ref[...], preferred_element_type=F32)
    vf = jnp.dot(ckvn, wv_ref[...], preferred_element_type=F32)
    krr = _rot(proj("kr"), cosq, slo, shi)
    lane = lax.broadcasted_iota(jnp.int32, (1, LANES), 1)
    ones_col = (lane == DV).astype(F32)
    for hh in range(HB):
        k_ref[hh] = (kf[:, hh * LANES:(hh + 1) * LANES] + krr).astype(BF16)
        v_ref[hh] = (vf[:, hh * LANES:(hh + 1) * LANES] + ones_col).T.astype(BF16)


def _proj_call(x2, seq, ln_g, w1, b_gate, q_norm_g, kv_norm_g, wq, wk, wv, tables, tm):
    t = x2.shape[0]
    nt = t // tm
    per_seq = seq // tm
    row = lambda i: (i, 0)
    const = lambda i: (0, 0)
    tab = lambda i: (i % per_seq, 0)
    head = lambda i: (0, i, 0)
    headt = lambda i: (0, 0, i)
    bs = pl.BlockSpec
    in_specs = [bs((tm, D_MODEL), row), bs((1, D_MODEL), const), bs((D_MODEL, W1_COLS), const),
                bs((1, 2 * D_MODEL), const), bs((1, Q_LORA), const), bs((1, KV_LORA), const),
                bs((Q_LORA, HB * LANES), const), bs((KV_LORA, HB * LANES), const),
                bs((KV_LORA, HB * LANES), const),
                bs((tm, LANES), tab), bs((tm, LANES), tab), bs((tm, LANES), tab)]
    out_shape = [jax.ShapeDtypeStruct((t, HA * LANES), BF16), jax.ShapeDtypeStruct((t, HKV_A * LANES), BF16),
                 jax.ShapeDtypeStruct((t, 2 * HKV_A * LANES), BF16),
                 jax.ShapeDtypeStruct((t, WIDTH_A), BF16), jax.ShapeDtypeStruct((t, WIDTH_B), BF16),
                 jax.ShapeDtypeStruct((t, D_MODEL), BF16), jax.ShapeDtypeStruct((t, D_MODEL), BF16),
                 jax.ShapeDtypeStruct((HB, LANES, t), BF16), jax.ShapeDtypeStruct((HB, t, LANES), BF16),
                 jax.ShapeDtypeStruct((HB, LANES, t), BF16)]
    out_specs = [bs((tm, HA * LANES), row), bs((tm, HKV_A * LANES), row), bs((tm, 2 * HKV_A * LANES), row),
                 bs((tm, WIDTH_A), row), bs((tm, WIDTH_B), row), bs((tm, D_MODEL), row), bs((tm, D_MODEL), row),
                 bs((HB, LANES, tm), headt), bs((HB, tm, LANES), head), bs((HB, LANES, tm), headt)]
    return pl.pallas_call(
        _proj_kernel, grid=(nt,), in_specs=in_specs, out_specs=out_specs, out_shape=out_shape,
        compiler_params=pltpu.CompilerParams(dimension_semantics=("arbitrary",), vmem_limit_bytes=VMEM_LIMIT),
        name="proj",
    )(x2, ln_g, w1, b_gate, q_norm_g, kv_norm_g, wq, wk, wv, *tables)


def _window_kernel(nb, relb_ref, sink_ref, bucket_ref, q_ref, kp_ref, kc_ref, kn_ref,
                   vp_ref, vc_ref, vn_ref, sza_ref, o_ref, bias_ref):
    first = (pl.program_id(0) == 0) & (pl.program_id(1) == 0)

    @pl.when(first)
    def _():
        bucket = bucket_ref[...]
        for hh in range(HA):
            acc = jnp.where(bucket < 0, NEG, 0.0).astype(F32)
            for b in range(NUM_BUCKETS):
                acc = acc + jnp.where(bucket == b, relb_ref[b, hh], 0.0)
            bias_ref[hh] = acc

    n = pl.program_id(1)
    col = lax.broadcasted_iota(jnp.int32, (1, 3 * BLK), 1)
    gone = ((col < BLK) & (n == 0)) | ((col >= 2 * BLK) & (n == nb - 1))
    pen = jnp.where(gone, NEG, 0.0).astype(F32)

    outs = []
    for g in range(HKV_A):
        ksl = slice(g * LANES, (g + 1) * LANES)
        kw = jnp.concatenate([kp_ref[:, ksl], kc_ref[:, ksl], kn_ref[:, ksl]], axis=0)
        for pair in range(G_A // 2):
            acc = None
            for odd in range(2):
                hh = g * G_A + pair * 2 + odd
                q = q_ref[:, hh * LANES:(hh + 1) * LANES]
                s = lax.dot_general(q, kw, (((1,), (1,)), ((), ())), preferred_element_type=F32)
                s = s + bias_ref[hh] + pen
                sk = sink_ref[hh]
                m = jnp.maximum(jnp.max(s, axis=-1, keepdims=True), sk)
                p = jnp.exp(s - m)
                denom = jnp.sum(p, axis=-1, keepdims=True) + jnp.exp(sk - m)
                pn = (p / denom).astype(BF16)
                vb = odd * HKV_A + g
                vsl = slice(vb * LANES, (vb + 1) * LANES)
                vw = jnp.concatenate([vp_ref[:, vsl], vc_ref[:, vsl], vn_ref[:, vsl]], axis=0)
                o = jnp.dot(pn, vw, preferred_element_type=F32)
                acc = o if acc is None else acc + o
            outs.append(acc)
    y = jnp.concatenate(outs, axis=1)
    o_ref[...] = (y * sza_ref[...].astype(F32)).astype(BF16)


def _window_call(qa, ka, va, sza, rel_bias, sink, bucket, batch, seq):
    t = qa.shape[0]
    nb = seq // BLK
    bs = pl.BlockSpec
    cur = lambda b, n: (b * nb + n, 0)
    prev = lambda b, n: (b * nb + jnp.maximum(n - 1, 0), 0)
    nxt = lambda b, n: (b * nb + jnp.minimum(n + 1, nb - 1), 0)
    smem = functools.partial(bs, memory_space=pltpu.SMEM)
    kw, vw = HKV_A * LANES, 2 * HKV_A * LANES
    in_specs = [smem(), smem(), bs((BLK, 3 * BLK), lambda b, n: (0, 0)),
                bs((BLK, HA * LANES), cur),
                bs((BLK, kw), prev), bs((BLK, kw), cur), bs((BLK, kw), nxt),
                bs((BLK, vw), prev), bs((BLK, vw), cur), bs((BLK, vw), nxt),
                bs((BLK, WIDTH_A), cur)]
    return pl.pallas_call(
        functools.partial(_window_kernel, nb),
        grid=(batch, nb), in_specs=in_specs, out_specs=bs((BLK, WIDTH_A), cur),
        out_shape=jax.ShapeDtypeStruct((t, WIDTH_A), BF16),
        scratch_shapes=[pltpu.VMEM((HA, BLK, 3 * BLK), F32)],
        compiler_params=pltpu.CompilerParams(dimension_semantics=("arbitrary", "arbitrary"),
                                             vmem_limit_bytes=VMEM_LIMIT),
        name="window",
    )(rel_bias, sink, bucket, qa, ka, ka, ka, va, va, va, sza)


def _mla_kernel(seq, bk, qt_ref, k_ref, vt_ref, szb_ref, o_ref, s_all, p_all, acc_all):
    bq = qt_ref.shape[2]
    nkv = seq // bk
    for hh in range(2):
        s_scr, p_scr, acc_scr = s_all.at[hh], p_all.at[hh], acc_all.at[hh]

        def scores(j, slot, hh=hh, s_scr=s_scr):
            start = pl.multiple_of(j * bk, bk)
            s = jnp.dot(k_ref[hh, pl.ds(start, bk), :], qt_ref[hh], preferred_element_type=F32)
            s_scr[slot] = s
            return jnp.max(s, axis=0, keepdims=True)

        def values(j, slot, alpha, hh=hh, p_scr=p_scr, acc_scr=acc_scr):
            start = pl.multiple_of(j * bk, bk)
            pv = jnp.dot(vt_ref[hh, :, pl.ds(start, bk)], p_scr[slot], preferred_element_type=F32)
            acc_scr[...] = acc_scr[...] * alpha + pv

        def softmax(slot, m_old, cmax, s_scr=s_scr, p_scr=p_scr):
            m_new = jnp.maximum(m_old, cmax)
            alpha = jnp.exp2(m_old - m_new)
            mb = jnp.broadcast_to(m_new, (MLA_ROWS, bq))
            for r in range(0, bk, MLA_ROWS):
                rows = slice(r, r + MLA_ROWS)
                p_scr[slot, rows, :] = jnp.exp2(s_scr[slot, rows, :] - mb).astype(BF16)
            return m_new, alpha

        cm0 = scores(0, 0)
        acc_scr[...] = jnp.zeros(acc_scr.shape, F32)
        p_scr[1] = jnp.zeros(p_scr.shape[1:], BF16)

        def group(t, carry, scores=scores, values=values, softmax=softmax):
            m, alpha, cm = carry
            for u in range(MLA_UNROLL):
                j = MLA_UNROLL * t + u
                slot = u % 2
                cm_next = scores(jnp.minimum(j + 1, nkv - 1), 1 - slot)
                values(jnp.maximum(j - 1, 0), 1 - slot, alpha)
                m, alpha = softmax(slot, m, cm)
                cm = cm_next
            return m, alpha, cm

        init = (jnp.full((1, bq), NEG, F32), jnp.ones((1, bq), F32), cm0)
        _, alpha, _ = lax.fori_loop(0, nkv // MLA_UNROLL, group, init)
        values(nkv - 1, 1, alpha)
        acc = acc_scr[...].T
        y = acc[:, :DV] / acc[:, DV:DV + 1]
        cols = slice(hh * DV, (hh + 1) * DV)
        o_ref[:, cols] = (y * szb_ref[:, cols].astype(F32)).astype(BF16)


def _mla_call(qt, k, vt, szb, batch, seq, bq, bk):
    t = k.shape[1]
    nq = seq // bq
    assert seq % (MLA_UNROLL * bk) == 0 and bk % MLA_ROWS == 0
    bs = pl.BlockSpec
    return pl.pallas_call(
        functools.partial(_mla_kernel, seq, bk),
        grid=(batch, HB // 2, nq),
        in_specs=[bs((2, LANES, bq), lambda b, hp, i: (hp, 0, b * nq + i)),
                  bs((2, seq, LANES), lambda b, hp, i: (hp, b, 0)),
                  bs((2, LANES, seq), lambda b, hp, i: (hp, 0, b)),
                  bs((bq, LANES), lambda b, hp, i: (b * nq + i, hp))],
        out_specs=bs((bq, LANES), lambda b, hp, i: (b * nq + i, hp)),
        out_shape=jax.ShapeDtypeStruct((t, WIDTH_B), BF16),
        scratch_shapes=[pltpu.VMEM((2, 2, bk, bq), F32), pltpu.VMEM((2, 2, bk, bq), BF16),
                        pltpu.VMEM((2, LANES, bq), F32)],
        compiler_params=pltpu.CompilerParams(dimension_semantics=("arbitrary", "arbitrary", "arbitrary"),
                                             vmem_limit_bytes=VMEM_LIMIT),
        name="mla",
    )(qt, k, vt, szb)


def _merge_kernel(x_ref, a_ref, b_ref, gga_ref, ggb_ref, wpa_ref, wpb_ref, wo_ref, fg_ref, o_ref):
    oa = jnp.dot(a_ref[...], wpa_ref[...], preferred_element_type=F32)
    ob = jnp.dot(b_ref[...], wpb_ref[...], preferred_element_type=F32)
    merged = gga_ref[...].astype(F32) * oa + ggb_ref[...].astype(F32) * ob
    y = x_ref[...] + jnp.dot(merged.astype(BF16), wo_ref[...], preferred_element_type=F32)
    o_ref[...] = _rms(y, fg_ref[...])


def _merge_call(x2, a, b, gga, ggb, wpa, wpb, wo, fg, tm):
    t = x2.shape[0]
    bs = pl.BlockSpec
    row = lambda i: (i, 0)
    const = lambda i: (0, 0)
    return pl.pallas_call(
        _merge_kernel, grid=(t // tm,),
        in_specs=[bs((tm, D_MODEL), row), bs((tm, WIDTH_A), row), bs((tm, WIDTH_B), row),
                  bs((tm, D_MODEL), row), bs((tm, D_MODEL), row),
                  bs((WIDTH_A, D_MODEL), const), bs((WIDTH_B, D_MODEL), const), bs((D_MODEL, D_MODEL), const),
                  bs((1, D_MODEL), const)],
        out_specs=bs((tm, D_MODEL), row),
        out_shape=jax.ShapeDtypeStruct((t, D_MODEL), F32),
        compiler_params=pltpu.CompilerParams(dimension_semantics=("arbitrary",), vmem_limit_bytes=VMEM_LIMIT),
        name="merge",
    )(x2, a, b, gga, ggb, wpa, wpb, wo, fg)


def _tiles(seq):
    tm = min(256, seq)
    bq = min(512, seq)
    bk = min(512, seq // MLA_UNROLL)
    return tm, bq, bk


def _trunk(x, p):
    batch, seq, _ = x.shape
    tm, bq, bk = _tiles(seq)
    x2 = x.reshape(batch * seq, D_MODEL)
    tables = _rope_tables(seq)
    (qa, ka, va, sza, szb, gga, ggb, q, k, v) = _proj_call(
        x2, seq, p["ln_g"], p["w1"], p["b_gate"], p["q_norm_g"], p["kv_norm_g"], p["wq"], p["wk"], p["wv"],
        tables, tm)
    a = _window_call(qa, ka, va, sza, p["rel_bias"], p["sink"], p["bucket"], batch, seq)
    b = _mla_call(q, k, v, szb, batch, seq, bq, bk)
    y = _merge_call(x2, a, b, gga, ggb, p["wpa"], p["wpb"], p["wo"], p["final_g"], tm)
    return y.reshape(batch, seq, D_MODEL)


def kernel(x_prompt, x_sample, ln_g, w_in, b_gate, sink_a, q_norm_g, kv_norm_g, w_uq, w_ukv,
           w_proj_a, w_proj_b, w_out, rel_bias, final_g):
    assert ln_g.shape[0] == 1, "one layer"
    w1, wq, wk, wv = _pack_weights(w_in[0], w_uq[0], w_ukv[0])
    p = dict(ln_g=ln_g[0][None, :], w1=w1, b_gate=b_gate[0][None, :], q_norm_g=q_norm_g[0][None, :],
             kv_norm_g=kv_norm_g[0][None, :], wq=wq, wk=wk, wv=wv,
             rel_bias=rel_bias.astype(F32), sink=sink_a[0].astype(F32), bucket=_bucket_table(),
             wpa=w_proj_a[0].astype(BF16), wpb=w_proj_b[0].astype(BF16), wo=w_out[0].astype(BF16),
             final_g=final_g[None, :])
    return (_trunk(x_prompt, p), _trunk(x_sample, p))
```

```python
import functools
import math

import jax
import jax.numpy as jnp
from jax import lax
from jax.experimental import pallas as pl
from jax.experimental.pallas import tpu as pltpu

D_MODEL = 1024
BLK = 128
WINDOW = 128
HA = 8
HKV_A = 2
G_A = HA // HKV_A
DH_A = 64
WIDTH_A = HA * DH_A
NUM_BUCKETS = 32
MAX_DISTANCE = 128
HB = 8
Q_LORA = 256
KV_LORA = 128
NOPE = 64
ROPE = 32
DV = 64
WIDTH_B = HB * DV
ROPE_THETA = 10000.0
EPS = 1e-6
NEG = -1e30
LOG2E = math.log2(math.e)
VA_ONES_LO = DH_A
VA_ONES_HI = 0
MLA_UNROLL = 16
MLA_ROWS = 16

LANES = 128
VMEM_LIMIT = 56 * 1024 * 1024

F32 = jnp.float32
BF16 = jnp.bfloat16

_W1_GROUPS = (("qa", HA * LANES), ("ka", HKV_A * LANES), ("va", 2 * HKV_A * LANES), ("za", WIDTH_A),
              ("cq", Q_LORA), ("ckv", KV_LORA), ("kr", LANES), ("zb", WIDTH_B),
              ("ga", D_MODEL), ("gb", D_MODEL))
_W1_OFF = {}
_o = 0
for _n, _w in _W1_GROUPS:
    _W1_OFF[_n] = (_o, _o + _w)
    _o += _w
W1_COLS = _o


def _pad_heads(w, n_heads, width, lane_off=0):
    k = w.shape[0]
    w = w.reshape(k, n_heads, width)
    out = jnp.zeros((k, n_heads, LANES), w.dtype)
    out = out.at[:, :, lane_off:lane_off + width].set(w)
    return out.reshape(k, n_heads * LANES)


def _pack_weights(w_in, w_uq, w_ukv):
    idx = [0]
    for s in (WIDTH_A, HKV_A * DH_A, HKV_A * DH_A, WIDTH_A, Q_LORA, KV_LORA, ROPE, WIDTH_B, D_MODEL, D_MODEL):
        idx.append(idx[-1] + s)
    qa, ka, va, za, cq, ckv, kr, zb, ga, gb = [w_in[:, idx[i]:idx[i + 1]] for i in range(10)]
    va_lo = _pad_heads(va, HKV_A, DH_A, 0)
    va_hi = _pad_heads(va, HKV_A, DH_A, DH_A)
    kr_p = jnp.zeros((D_MODEL, LANES), w_in.dtype).at[:, NOPE:NOPE + ROPE].set(kr)
    w1 = jnp.concatenate([_pad_heads(qa, HA, DH_A), _pad_heads(ka, HKV_A, DH_A), va_lo, va_hi, za,
                          cq, ckv, kr_p, zb, ga, gb], axis=1).astype(BF16)
    wq = _pad_heads(w_uq, HB, NOPE + ROPE).astype(BF16)
    kvf = w_ukv.reshape(KV_LORA, HB, NOPE + DV)
    wk = _pad_heads(kvf[:, :, :NOPE].reshape(KV_LORA, HB * NOPE), HB, NOPE).astype(BF16)
    wv = _pad_heads(kvf[:, :, NOPE:].reshape(KV_LORA, HB * DV), HB, DV).astype(BF16)
    return w1, wq, wk, wv


def _rope_tables(seq):
    half = ROPE // 2
    inv = jnp.power(ROPE_THETA, -jnp.arange(half, dtype=F32) / half)
    ang = jnp.arange(seq, dtype=F32)[:, None] * inv[None, :]
    cos, sin = jnp.cos(ang), jnp.sin(ang)
    z = jnp.zeros((seq, half), F32)
    pad = jnp.zeros((seq, LANES - NOPE - ROPE), F32)
    ones = jnp.ones((seq, NOPE), F32)
    zn = jnp.zeros((seq, NOPE), F32)
    cosq = jnp.concatenate([ones, cos, cos, pad], axis=1)
    sin_lo = jnp.concatenate([zn, -sin, z, pad], axis=1)
    sin_hi = jnp.concatenate([zn, z, sin, pad], axis=1)
    return cosq, sin_lo, sin_hi


def _bucket_table():
    qi = jnp.arange(BLK, dtype=jnp.int32)[:, None]
    kj = jnp.arange(3 * BLK, dtype=jnp.int32)[None, :]
    rel = kj - BLK - qi
    nb = NUM_BUCKETS // 2
    max_exact = nb // 2
    ret = (rel > 0).astype(jnp.int32) * nb
    n = jnp.abs(rel)
    nf = jnp.maximum(n, 1).astype(F32)
    large = max_exact + (jnp.log(nf / max_exact) / math.log(MAX_DISTANCE / max_exact)
                         * (nb - max_exact)).astype(jnp.int32)
    large = jnp.minimum(large, nb - 1)
    bucket = ret + jnp.where(n < max_exact, n, large)
    return jnp.where(n <= WINDOW, bucket, -1)


def _rms(v, g):
    return v * lax.rsqrt(jnp.mean(v * v, axis=-1, keepdims=True) + EPS) * g


def _sigmoid(v):
    return 1.0 / (1.0 + jnp.exp(-v))


def _rot(v, cosq, sin_lo, sin_hi):
    return v * cosq + pltpu.roll(v, LANES - ROPE // 2, 1) * sin_lo + pltpu.roll(v, ROPE // 2, 1) * sin_hi


def _proj_kernel(x_ref, lng_ref, w1_ref, bg_ref, qg_ref, kvg_ref, wq_ref, wk_ref, wv_ref,
                 cos_ref, slo_ref, shi_ref,
                 qa_ref, ka_ref, va_ref, sza_ref, szb_ref, gga_ref, ggb_ref, q_ref, k_ref, v_ref):
    x = x_ref[...]
    h = _rms(x, lng_ref[...]).astype(BF16)

    def proj(name):
        a, b = _W1_OFF[name]
        return jnp.dot(h, w1_ref[:, a:b], preferred_element_type=F32)

    qa_ref[...] = (proj("qa") * (DH_A ** -0.5 * LOG2E)).astype(BF16)
    ka_ref[...] = proj("ka").astype(BF16)
    lane_a = lax.broadcasted_iota(jnp.int32, (1, 2 * HKV_A * LANES), 1)
    ones_at = jnp.where(lane_a < HKV_A * LANES, VA_ONES_LO, VA_ONES_HI)
    va_ones = (lane_a % LANES == ones_at).astype(F32)
    va_ref[...] = (proj("va") + va_ones).astype(BF16)
    za = proj("za")
    sza_ref[...] = (za * _sigmoid(za)).astype(BF16)
    zb = proj("zb")
    szb_ref[...] = (zb * _sigmoid(zb)).astype(BF16)
    gga_ref[...] = _sigmoid(proj("ga") + bg_ref[:, :D_MODEL]).astype(BF16)
    ggb_ref[...] = _sigmoid(proj("gb") + bg_ref[:, D_MODEL:]).astype(BF16)

    cosq, slo, shi = cos_ref[...], slo_ref[...], shi_ref[...]
    cqn = _rms(proj("cq"), qg_ref[...]).astype(BF16)
    qf = jnp.dot(cqn, wq_ref[...], preferred_element_type=F32)
    scale = (NOPE + ROPE) ** -0.5 * LOG2E
    for hh in range(HB):
        blk = qf[:, hh * LANES:(hh + 1) * LANES]
        q_ref[hh] = (_rot(blk, cosq, slo, shi) * scale).T.astype(BF16)

    ckvn = _rms(proj("ckv"), kvg_ref[...]).astype(BF16)
    kf = jnp.dot(ckvn, wk_ref[...], preferred_element_type=F32)
    vf = jnp.dot(ckvn, wv_ref[...], preferred_element_type=F32)
    krr = _rot(proj("kr"), cosq, slo, shi)
    lane = lax.broadcasted_iota(jnp.int32, (1, LANES), 1)
    ones_col = (lane == DV).astype(F32)
    for hh in range(HB):
        k_ref[hh] = (kf[:, hh * LANES:(hh + 1) * LANES] + krr).astype(BF16)
        v_ref[hh] = (vf[:, hh * LANES:(hh + 1) * LANES] + ones_col).T.astype(BF16)


def _proj_call(x2, seq, ln_g, w1, b_gate, q_norm_g, kv_norm_g, wq, wk, wv, tables, tm):
    t = x2.shape[0]
    nt = t // tm
    per_seq = seq // tm
    row = lambda i: (i, 0)
    const = lambda i: (0, 0)
    tab = lambda i: (i % per_seq, 0)
    head = lambda i: (0, i, 0)
    headt = lambda i: (0, 0, i)
    bs = pl.BlockSpec
    in_specs = [bs((tm, D_MODEL), row), bs((1, D_MODEL), const), bs((D_MODEL, W1_COLS), const),
                bs((1, 2 * D_MODEL), const), bs((1, Q_LORA), const), bs((1, KV_LORA), const),
                bs((Q_LORA, HB * LANES), const), bs((KV_LORA, HB * LANES), const),
                bs((KV_LORA, HB * LANES), const),
                bs((tm, LANES), tab), bs((tm, LANES), tab), bs((tm, LANES), tab)]
    out_shape = [jax.ShapeDtypeStruct((t, HA * LANES), BF16), jax.ShapeDtypeStruct((t, HKV_A * LANES), BF16),
                 jax.ShapeDtypeStruct((t, 2 * HKV_A * LANES), BF16),
                 jax.ShapeDtypeStruct((t, WIDTH_A), BF16), jax.ShapeDtypeStruct((t, WIDTH_B), BF16),
                 jax.ShapeDtypeStruct((t, D_MODEL), BF16), jax.ShapeDtypeStruct((t, D_MODEL), BF16),
                 jax.ShapeDtypeStruct((HB, LANES, t), BF16), jax.ShapeDtypeStruct((HB, t, LANES), BF16),
                 jax.ShapeDtypeStruct((HB, LANES, t), BF16)]
    out_specs = [bs((tm, HA * LANES), row), bs((tm, HKV_A * LANES), row), bs((tm, 2 * HKV_A * LANES), row),
                 bs((tm, WIDTH_A), row), bs((tm, WIDTH_B), row), bs((tm, D_MODEL), row), bs((tm, D_MODEL), row),
                 bs((HB, LANES, tm), headt), bs((HB, tm, LANES), head), bs((HB, LANES, tm), headt)]
    return pl.pallas_call(
        _proj_kernel, grid=(nt,), in_specs=in_specs, out_specs=out_specs, out_shape=out_shape,
        compiler_params=pltpu.CompilerParams(dimension_semantics=("arbitrary",), vmem_limit_bytes=VMEM_LIMIT),
        name="proj",
    )(x2, ln_g, w1, b_gate, q_norm_g, kv_norm_g, wq, wk, wv, *tables)


def _window_kernel(nb, relb_ref, sink_ref, bucket_ref, q_ref, kp_ref, kc_ref, kn_ref,
                   vp_ref, vc_ref, vn_ref, sza_ref, o_ref, bias_ref, s_ref):
    first = (pl.program_id(0) == 0) & (pl.program_id(1) == 0)

    @pl.when(first)
    def _():
        bucket = bucket_ref[...]
        col = lax.broadcasted_iota(jnp.int32, (1, 3 * BLK), 1)
        for hh in range(HA):
            acc = jnp.where(bucket < 0, NEG, 0.0).astype(F32)
            for b in range(NUM_BUCKETS):
                acc = acc + jnp.where(bucket == b, relb_ref[b, hh] * LOG2E, 0.0)
            for e in range(4):
                gone = ((col < BLK) & (e % 2 == 1)) | ((col >= 2 * BLK) & (e // 2 == 1))
                bias_ref[e, hh] = acc + jnp.where(gone, NEG, 0.0).astype(F32)

    n = pl.program_id(1)
    edge = (n == 0).astype(jnp.int32) + 2 * (n == nb - 1).astype(jnp.int32)
    lane = lax.broadcasted_iota(jnp.int32, (1, LANES), 1)

    for hh in range(HA):
        ksl = slice((hh // G_A) * LANES, (hh // G_A + 1) * LANES)
        kw = jnp.concatenate([kp_ref[:, ksl], kc_ref[:, ksl], kn_ref[:, ksl]], axis=0)
        q = q_ref[:, hh * LANES:(hh + 1) * LANES]
        s_ref[hh] = lax.dot_general(q, kw, (((1,), (1,)), ((), ())), preferred_element_type=F32)

    halves = []
    for hh in range(HA):
        s = s_ref[hh] + bias_ref[edge, hh]
        sk = sink_ref[hh] * LOG2E
        m = jnp.maximum(jnp.max(s, axis=-1, keepdims=True), sk)
        p = jnp.exp2(s - m).astype(BF16)
        vb = (hh % 2) * HKV_A + hh // G_A
        vsl = slice(vb * LANES, (vb + 1) * LANES)
        vw = jnp.concatenate([vp_ref[:, vsl], vc_ref[:, vsl], vn_ref[:, vsl]], axis=0)
        o = jnp.dot(p, vw, preferred_element_type=F32)
        ones_lane = VA_ONES_HI if hh % 2 else VA_ONES_LO
        denom = o[:, ones_lane:ones_lane + 1] + jnp.exp2(sk - m)
        halves.append(o / denom)
    outs = [jnp.where(lane < DH_A, halves[2 * i], halves[2 * i + 1]) for i in range(HA // 2)]
    y = jnp.concatenate(outs, axis=1)
    o_ref[...] = (y * sza_ref[...].astype(F32)).astype(BF16)


def _window_call(qa, ka, va, sza, rel_bias, sink, bucket, batch, seq):
    t = qa.shape[0]
    nb = seq // BLK
    bs = pl.BlockSpec
    cur = lambda b, n: (b * nb + n, 0)
    prev = lambda b, n: (b * nb + jnp.maximum(n - 1, 0), 0)
    nxt = lambda b, n: (b * nb + jnp.minimum(n + 1, nb - 1), 0)
    smem = functools.partial(bs, memory_space=pltpu.SMEM)
    kw, vw = HKV_A * LANES, 2 * HKV_A * LANES
    in_specs = [smem(), smem(), bs((BLK, 3 * BLK), lambda b, n: (0, 0)),
                bs((BLK, HA * LANES), cur),
                bs((BLK, kw), prev), bs((BLK, kw), cur), bs((BLK, kw), nxt),
                bs((BLK, vw), prev), bs((BLK, vw), cur), bs((BLK, vw), nxt),
                bs((BLK, WIDTH_A), cur)]
    return pl.pallas_call(
        functools.partial(_window_kernel, nb),
        grid=(batch, nb), in_specs=in_specs, out_specs=bs((BLK, WIDTH_A), cur),
        out_shape=jax.ShapeDtypeStruct((t, WIDTH_A), BF16),
        scratch_shapes=[pltpu.VMEM((4, HA, BLK, 3 * BLK), F32), pltpu.VMEM((HA, BLK, 3 * BLK), F32)],
        compiler_params=pltpu.CompilerParams(dimension_semantics=("arbitrary", "arbitrary"),
                                             vmem_limit_bytes=VMEM_LIMIT),
        name="window",
    )(rel_bias, sink, bucket, qa, ka, ka, ka, va, va, va, sza)


def _mla_kernel(seq, bk, qt_ref, k_ref, vt_ref, szb_ref, o_ref, s_all, p_all, acc_all):
    bq = qt_ref.shape[2]
    nkv = seq // bk
    unroll = min(MLA_UNROLL, nkv)
    assert unroll % 2 == 0 and nkv % unroll == 0
    for hh in range(2):
        s_scr, p_scr, acc_scr = s_all.at[hh], p_all.at[hh], acc_all.at[hh]

        def scores(j, slot, hh=hh, s_scr=s_scr):
            start = pl.multiple_of(j * bk, bk)
            s = jnp.dot(k_ref[hh, pl.ds(start, bk), :], qt_ref[hh], preferred_element_type=F32)
            s_scr[slot] = s
            return jnp.max(s, axis=0, keepdims=True)

        def values(j, slot, alpha, hh=hh, p_scr=p_scr, acc_scr=acc_scr):
            start = pl.multiple_of(j * bk, bk)
            pv = jnp.dot(vt_ref[hh, :, pl.ds(start, bk)], p_scr[slot], preferred_element_type=F32)
            acc_scr[...] = acc_scr[...] * alpha + pv

        def softmax(slot, m_old, cmax, s_scr=s_scr, p_scr=p_scr):
            m_new = jnp.maximum(m_old, cmax)
            alpha = jnp.exp2(m_old - m_new)
            mb = jnp.broadcast_to(m_new, (MLA_ROWS, bq))
            for r in range(0, bk, MLA_ROWS):
                rows = slice(r, r + MLA_ROWS)
                p_scr[slot, rows, :] = jnp.exp2(s_scr[slot, rows, :] - mb).astype(BF16)
            return m_new, alpha

        cm0 = scores(0, 0)
        acc_scr[...] = jnp.zeros(acc_scr.shape, F32)
        p_scr[1] = jnp.zeros(p_scr.shape[1:], BF16)

        def group(t, carry, scores=scores, values=values, softmax=softmax):
            m, alpha, cm = carry
            for u in range(unroll):
                j = unroll * t + u
                slot = u % 2
                cm_next = scores(jnp.minimum(j + 1, nkv - 1), 1 - slot)
                values(jnp.maximum(j - 1, 0), 1 - slot, alpha)
                m, alpha = softmax(slot, m, cm)
                cm = cm_next
            return m, alpha, cm

        init = (jnp.full((1, bq), NEG, F32), jnp.ones((1, bq), F32), cm0)
        _, alpha, _ = lax.fori_loop(0, nkv // unroll, group, init)
        values(nkv - 1, 1, alpha)
        acc = acc_scr[...].T
        y = acc[:, :DV] / acc[:, DV:DV + 1]
        cols = slice(hh * DV, (hh + 1) * DV)
        o_ref[:, cols] = (y * szb_ref[:, cols].astype(F32)).astype(BF16)


def _mla_call(qt, k, vt, szb, batch, seq, bq, bk):
    t = k.shape[1]
    nq = seq // bq
    assert seq % (2 * bk) == 0 and bk % MLA_ROWS == 0
    bs = pl.BlockSpec
    return pl.pallas_call(
        functools.partial(_mla_kernel, seq, bk),
        grid=(batch, HB // 2, nq),
        in_specs=[bs((2, LANES, bq), lambda b, hp, i: (hp, 0, b * nq + i)),
                  bs((2, seq, LANES), lambda b, hp, i: (hp, b, 0)),
                  bs((2, LANES, seq), lambda b, hp, i: (hp, 0, b)),
                  bs((bq, LANES), lambda b, hp, i: (b * nq + i, hp))],
        out_specs=bs((bq, LANES), lambda b, hp, i: (b * nq + i, hp)),
        out_shape=jax.ShapeDtypeStruct((t, WIDTH_B), BF16),
        scratch_shapes=[pltpu.VMEM((2, 2, bk, bq), F32), pltpu.VMEM((2, 2, bk, bq), BF16),
                        pltpu.VMEM((2, LANES, bq), F32)],
        compiler_params=pltpu.CompilerParams(dimension_semantics=("arbitrary", "arbitrary", "arbitrary"),
                                             vmem_limit_bytes=VMEM_LIMIT),
        name="mla",
    )(qt, k, vt, szb)


def _merge_kernel(x_ref, a_ref, b_ref, gga_ref, ggb_ref, wpa_ref, wpb_ref, wo_ref, fg_ref, o_ref):
    oa = jnp.dot(a_ref[...], wpa_ref[...], preferred_element_type=F32)
    ob = jnp.dot(b_ref[...], wpb_ref[...], preferred_element_type=F32)
    merged = gga_ref[...].astype(F32) * oa + ggb_ref[...].astype(F32) * ob
    y = x_ref[...] + jnp.dot(merged.astype(BF16), wo_ref[...], preferred_element_type=F32)
    o_ref[...] = _rms(y, fg_ref[...])


def _merge_call(x2, a, b, gga, ggb, wpa, wpb, wo, fg, tm):
    t = x2.shape[0]
    bs = pl.BlockSpec
    row = lambda i: (i, 0)
    const = lambda i: (0, 0)
    return pl.pallas_call(
        _merge_kernel, grid=(t // tm,),
        in_specs=[bs((tm, D_MODEL), row), bs((tm, WIDTH_A), row), bs((tm, WIDTH_B), row),
                  bs((tm, D_MODEL), row), bs((tm, D_MODEL), row),
                  bs((WIDTH_A, D_MODEL), const), bs((WIDTH_B, D_MODEL), const), bs((D_MODEL, D_MODEL), const),
                  bs((1, D_MODEL), const)],
        out_specs=bs((tm, D_MODEL), row),
        out_shape=jax.ShapeDtypeStruct((t, D_MODEL), F32),
        compiler_params=pltpu.CompilerParams(dimension_semantics=("arbitrary",), vmem_limit_bytes=VMEM_LIMIT),
        name="merge",
    )(x2, a, b, gga, ggb, wpa, wpb, wo, fg)


def _tiles(seq):
    tm = min(256, seq)
    bq = min(512, seq)
    bk = min(256, seq // 2)
    return tm, bq, bk


def _trunk(x, p):
    batch, seq, _ = x.shape
    tm, bq, bk = _tiles(seq)
    x2 = x.reshape(batch * seq, D_MODEL)
    tables = _rope_tables(seq)
    (qa, ka, va, sza, szb, gga, ggb, q, k, v) = _proj_call(
        x2, seq, p["ln_g"], p["w1"], p["b_gate"], p["q_norm_g"], p["kv_norm_g"], p["wq"], p["wk"], p["wv"],
        tables, tm)
    a = _window_call(qa, ka, va, sza, p["rel_bias"], p["sink"], p["bucket"], batch, seq)
    b = _mla_call(q, k, v, szb, batch, seq, bq, bk)
    y = _merge_call(x2, a, b, gga, ggb, p["wpa"], p["wpb"], p["wo"], p["final_g"], tm)
    return y.reshape(batch, seq, D_MODEL)


def kernel(x_prompt, x_sample, ln_g, w_in, b_gate, sink_a, q_norm_g, kv_norm_g, w_uq, w_ukv,
           w_proj_a, w_proj_b, w_out, rel_bias, final_g):
    assert ln_g.shape[0] == 1, "one layer"
    w1, wq, wk, wv = _pack_weights(w_in[0], w_uq[0], w_ukv[0])
    p = dict(ln_g=ln_g[0][None, :], w1=w1, b_gate=b_gate[0][None, :], q_norm_g=q_norm_g[0][None, :],
             kv_norm_g=kv_norm_g[0][None, :], wq=wq, wk=wk, wv=wv,
             rel_bias=rel_bias.astype(F32), sink=sink_a[0].astype(F32), bucket=_bucket_table(),
             wpa=w_proj_a[0].astype(BF16), wpb=w_proj_b[0].astype(BF16), wo=w_out[0].astype(BF16),
             final_g=final_g[None, :])
    return (_trunk(x_prompt, p), _trunk(x_sample, p))
```

```python
import functools
import math

import jax
import jax.numpy as jnp
from jax import lax
from jax.experimental import pallas as pl
from jax.experimental.pallas import tpu as pltpu

D_MODEL = 1024
BLK = 128
WINDOW = 128
HA = 8
HKV_A = 2
G_A = HA // HKV_A
DH_A = 64
WIDTH_A = HA * DH_A
NUM_BUCKETS = 32
MAX_DISTANCE = 128
HB = 8
Q_LORA = 256
KV_LORA = 128
NOPE = 64
ROPE = 32
DV = 64
WIDTH_B = HB * DV
ROPE_THETA = 10000.0
EPS = 1e-6
NEG = -1e30
LOG2E = math.log2(math.e)
MLA_UNROLL = 32
DV_ROWS = 80
MLA_ROWS = 16

LANES = 128
VMEM_LIMIT = 56 * 1024 * 1024

F32 = jnp.float32
BF16 = jnp.bfloat16

_W1_GROUPS = (("qa", WIDTH_A), ("kva", 2 * HKV_A * DH_A), ("za", WIDTH_A),
              ("cq", Q_LORA), ("ckv", KV_LORA), ("kr", LANES), ("zb", WIDTH_B),
              ("ga", D_MODEL), ("gb", D_MODEL))
_W1_OFF = {}
_o = 0
for _n, _w in _W1_GROUPS:
    _W1_OFF[_n] = (_o, _o + _w)
    _o += _w
W1_COLS = _o


def _pad_heads(w, n_heads, width):
    k = w.shape[0]
    w = w.reshape(k, n_heads, width)
    out = jnp.zeros((k, n_heads, LANES), w.dtype)
    out = out.at[:, :, :width].set(w)
    return out.reshape(k, n_heads * LANES)


def _pack_weights(w_in, w_uq, w_ukv):
    idx = [0]
    for s in (WIDTH_A, HKV_A * DH_A, HKV_A * DH_A, WIDTH_A, Q_LORA, KV_LORA, ROPE, WIDTH_B, D_MODEL, D_MODEL):
        idx.append(idx[-1] + s)
    qa, ka, va, za, cq, ckv, kr, zb, ga, gb = [w_in[:, idx[i]:idx[i + 1]] for i in range(10)]
    kr_p = jnp.zeros((D_MODEL, LANES), w_in.dtype).at[:, NOPE:NOPE + ROPE].set(kr)
    w1 = jnp.concatenate([qa, ka, va, za, cq, ckv, kr_p, zb, ga, gb], axis=1).astype(BF16)
    wq = _pad_heads(w_uq, HB, NOPE + ROPE).astype(BF16)
    kvf = w_ukv.reshape(KV_LORA, HB, NOPE + DV)
    wk = _pad_heads(kvf[:, :, :NOPE].reshape(KV_LORA, HB * NOPE), HB, NOPE).astype(BF16)
    wv = _pad_heads(kvf[:, :, NOPE:].reshape(KV_LORA, HB * DV), HB, DV).astype(BF16)
    return w1, wq, wk, wv


def _rope_tables(seq):
    half = ROPE // 2
    inv = jnp.power(ROPE_THETA, -jnp.arange(half, dtype=F32) / half)
    ang = jnp.arange(seq, dtype=F32)[:, None] * inv[None, :]
    cos, sin = jnp.cos(ang), jnp.sin(ang)
    z = jnp.zeros((seq, half), F32)
    pad = jnp.zeros((seq, LANES - NOPE - ROPE), F32)
    ones = jnp.ones((seq, NOPE), F32)
    zn = jnp.zeros((seq, NOPE), F32)
    cosq = jnp.concatenate([ones, cos, cos, pad], axis=1)
    sin_lo = jnp.concatenate([zn, -sin, z, pad], axis=1)
    sin_hi = jnp.concatenate([zn, z, sin, pad], axis=1)
    return cosq, sin_lo, sin_hi


def _window_bias(rel_bias):
    qi = jnp.arange(BLK, dtype=jnp.int32)[:, None]
    kj = jnp.arange(3 * BLK, dtype=jnp.int32)[None, :]
    rel = kj - BLK - qi
    nb = NUM_BUCKETS // 2
    max_exact = nb // 2
    ret = (rel > 0).astype(jnp.int32) * nb
    n = jnp.abs(rel)
    nf = jnp.maximum(n, 1).astype(F32)
    large = max_exact + (jnp.log(nf / max_exact) / math.log(MAX_DISTANCE / max_exact)
                         * (nb - max_exact)).astype(jnp.int32)
    large = jnp.minimum(large, nb - 1)
    bucket = ret + jnp.where(n < max_exact, n, large)
    bias = rel_bias.astype(F32)[bucket].transpose(2, 0, 1) * LOG2E
    variants = []
    for e in range(4):
        gone = ((kj < BLK) & (e % 2 == 1)) | ((kj >= 2 * BLK) & (e // 2 == 1))
        variants.append(jnp.where((n <= WINDOW) & ~gone, bias, NEG))
    return jnp.stack(variants)


def _rms(v, g):
    return v * lax.rsqrt(jnp.mean(v * v, axis=-1, keepdims=True) + EPS) * g


def _sigmoid(v):
    return 1.0 / (1.0 + jnp.exp(-v))


def _rot(v, cosq, sin_lo, sin_hi):
    return v * cosq + pltpu.roll(v, LANES - ROPE // 2, 1) * sin_lo + pltpu.roll(v, ROPE // 2, 1) * sin_hi


def _proj_kernel(x_ref, lng_ref, w1_ref, bg_ref, qg_ref, kvg_ref, wq_ref, wk_ref, wv_ref,
                 cos_ref, slo_ref, shi_ref,
                 qa_ref, kva_ref, sza_ref, szb_ref, gga_ref, ggb_ref, q_ref, k_ref, v_ref):
    x = x_ref[...]
    h = _rms(x, lng_ref[...]).astype(BF16)

    def proj(name):
        a, b = _W1_OFF[name]
        return jnp.dot(h, w1_ref[:, a:b], preferred_element_type=F32)

    qa_ref[...] = (proj("qa") * (DH_A ** -0.5 * LOG2E)).astype(BF16)
    kva_ref[...] = proj("kva").astype(BF16)
    za = proj("za")
    sza_ref[...] = (za * _sigmoid(za)).astype(BF16)
    zb = proj("zb")
    szb_ref[...] = (zb * _sigmoid(zb)).astype(BF16)
    gga_ref[...] = _sigmoid(proj("ga") + bg_ref[:, :D_MODEL]).astype(BF16)
    ggb_ref[...] = _sigmoid(proj("gb") + bg_ref[:, D_MODEL:]).astype(BF16)

    cosq, slo, shi = cos_ref[...], slo_ref[...], shi_ref[...]
    cqn = _rms(proj("cq"), qg_ref[...]).astype(BF16)
    qf = jnp.dot(cqn, wq_ref[...], preferred_element_type=F32)
    scale = (NOPE + ROPE) ** -0.5 * LOG2E
    for hh in range(HB):
        blk = qf[:, hh * LANES:(hh + 1) * LANES]
        q_ref[hh] = (_rot(blk, cosq, slo, shi) * scale).T.astype(BF16)

    ckvn = _rms(proj("ckv"), kvg_ref[...]).astype(BF16)
    kf = jnp.dot(ckvn, wk_ref[...], preferred_element_type=F32)
    vf = jnp.dot(ckvn, wv_ref[...], preferred_element_type=F32)
    krr = _rot(proj("kr"), cosq, slo, shi)
    lane = lax.broadcasted_iota(jnp.int32, (1, LANES), 1)
    ones_col = (lane == DV).astype(F32)
    for hh in range(HB):
        k_ref[hh] = (kf[:, hh * LANES:(hh + 1) * LANES] + krr).astype(BF16)
        v_ref[hh] = (vf[:, hh * LANES:(hh + 1) * LANES] + ones_col).T[:DV_ROWS].astype(BF16)


def _proj_call(x2, seq, ln_g, w1, b_gate, q_norm_g, kv_norm_g, wq, wk, wv, tables, tm):
    t = x2.shape[0]
    nt = t // tm
    per_seq = seq // tm
    row = lambda i: (i, 0)
    const = lambda i: (0, 0)
    tab = lambda i: (i % per_seq, 0)
    head = lambda i: (0, i, 0)
    headt = lambda i: (0, 0, i)
    bs = pl.BlockSpec
    kva = 2 * HKV_A * DH_A
    in_specs = [bs((tm, D_MODEL), row), bs((1, D_MODEL), const), bs((D_MODEL, W1_COLS), const, pipeline_mode=pl.Buffered(1)),
                bs((1, 2 * D_MODEL), const), bs((1, Q_LORA), const), bs((1, KV_LORA), const),
                bs((Q_LORA, HB * LANES), const), bs((KV_LORA, HB * LANES), const),
                bs((KV_LORA, HB * LANES), const),
                bs((tm, LANES), tab), bs((tm, LANES), tab), bs((tm, LANES), tab)]
    out_shape = [jax.ShapeDtypeStruct((t, WIDTH_A), BF16), jax.ShapeDtypeStruct((t, kva), BF16),
                 jax.ShapeDtypeStruct((t, WIDTH_A), BF16), jax.ShapeDtypeStruct((t, WIDTH_B), BF16),
                 jax.ShapeDtypeStruct((t, D_MODEL), BF16), jax.ShapeDtypeStruct((t, D_MODEL), BF16),
                 jax.ShapeDtypeStruct((HB, LANES, t), BF16), jax.ShapeDtypeStruct((HB, t, LANES), BF16),
                 jax.ShapeDtypeStruct((HB, DV_ROWS, t), BF16)]
    out_specs = [bs((tm, WIDTH_A), row), bs((tm, kva), row),
                 bs((tm, WIDTH_A), row), bs((tm, WIDTH_B), row), bs((tm, D_MODEL), row), bs((tm, D_MODEL), row),
                 bs((HB, LANES, tm), headt), bs((HB, tm, LANES), head), bs((HB, DV_ROWS, tm), headt)]
    return pl.pallas_call(
        _proj_kernel, grid=(nt,), in_specs=in_specs, out_specs=out_specs, out_shape=out_shape,
        compiler_params=pltpu.CompilerParams(dimension_semantics=("arbitrary",), vmem_limit_bytes=VMEM_LIMIT),
        name="proj",
    )(x2, ln_g, w1, b_gate, q_norm_g, kv_norm_g, wq, wk, wv, *tables)


def _window_kernel(nb, sink_ref, bias_ref, q_ref, kvp_ref, kvc_ref, kvn_ref, sza_ref, o_ref, s_ref):
    n = pl.program_id(1)
    edge = (n == 0).astype(jnp.int32) + 2 * (n == nb - 1).astype(jnp.int32)
    lane = lax.broadcasted_iota(jnp.int32, (1, LANES), 1)

    kv = jnp.concatenate([kvp_ref[...], kvc_ref[...], kvn_ref[...]], axis=0)
    vw = jnp.concatenate([kv[:, HKV_A * DH_A:], jnp.ones((3 * BLK, LANES), BF16)], axis=1)

    for hh in range(HA):
        g = hh // G_A
        s_ref[hh] = lax.dot_general(q_ref[:, hh * DH_A:(hh + 1) * DH_A], kv[:, g * DH_A:(g + 1) * DH_A],
                                    (((1,), (1,)), ((), ())), preferred_element_type=F32)

    halves = []
    for hh in range(HA):
        s = s_ref[hh] + bias_ref[edge, hh]
        sk = sink_ref[hh] * LOG2E
        m = jnp.maximum(jnp.max(s, axis=-1, keepdims=True), sk)
        p = jnp.exp2(s - m).astype(BF16)
        o = jnp.dot(p, vw, preferred_element_type=F32)
        y_h = o[:, :LANES] / (o[:, LANES:] + jnp.exp2(sk - m))
        halves.append(y_h if (hh // G_A) == (hh % 2) else pltpu.roll(y_h, DH_A, 1))
    outs = [jnp.where(lane < DH_A, halves[2 * i], halves[2 * i + 1]) for i in range(HA // 2)]
    y = jnp.concatenate(outs, axis=1)
    o_ref[...] = (y * sza_ref[...].astype(F32)).astype(BF16)


def _window_call(qa, kva, sza, bias, sink, batch, seq):
    t = qa.shape[0]
    nb = seq // BLK
    bs = pl.BlockSpec
    cur = lambda b, n: (b * nb + n, 0)
    prev = lambda b, n: (b * nb + jnp.maximum(n - 1, 0), 0)
    nxt = lambda b, n: (b * nb + jnp.minimum(n + 1, nb - 1), 0)
    smem = functools.partial(bs, memory_space=pltpu.SMEM)
    kvw = 2 * HKV_A * DH_A
    in_specs = [smem(), bs((4, HA, BLK, 3 * BLK), lambda b, n: (0, 0, 0, 0), pipeline_mode=pl.Buffered(1)),
                bs((BLK, WIDTH_A), cur),
                bs((BLK, kvw), prev), bs((BLK, kvw), cur), bs((BLK, kvw), nxt),
                bs((BLK, WIDTH_A), cur)]
    return pl.pallas_call(
        functools.partial(_window_kernel, nb),
        grid=(batch, nb), in_specs=in_specs, out_specs=bs((BLK, WIDTH_A), cur),
        out_shape=jax.ShapeDtypeStruct((t, WIDTH_A), BF16),
        scratch_shapes=[pltpu.VMEM((HA, BLK, 3 * BLK), F32)],
        compiler_params=pltpu.CompilerParams(dimension_semantics=("arbitrary", "arbitrary"),
                                             vmem_limit_bytes=VMEM_LIMIT),
        name="window",
    )(sink, bias, qa, kva, kva, kva, sza)


def _mla_kernel(seq, bk, qt_ref, k_ref, vt_ref, szb_ref, o_ref, s_all, p_all, acc_all):
    bq = qt_ref.shape[2]
    nkv = seq // bk
    unroll = min(MLA_UNROLL, nkv)
    assert unroll % 2 == 0 and nkv % unroll == 0
    for hh in range(2):
        s_scr, p_scr, acc_scr = s_all.at[hh], p_all.at[hh], acc_all.at[hh]

        def scores(j, slot, hh=hh, s_scr=s_scr):
            start = pl.multiple_of(j * bk, bk)
            s = jnp.dot(k_ref[hh, pl.ds(start, bk), :], qt_ref[hh], preferred_element_type=F32)
            s_scr[slot] = s
            return jnp.max(s, axis=0, keepdims=True)

        def values(j, slot, alpha, hh=hh, p_scr=p_scr, acc_scr=acc_scr):
            start = pl.multiple_of(j * bk, bk)
            pv = jnp.dot(vt_ref[hh, :, pl.ds(start, bk)], p_scr[slot], preferred_element_type=F32)
            acc_scr[...] = acc_scr[...] * alpha + pv

        def softmax(slot, m_old, cmax, s_scr=s_scr, p_scr=p_scr):
            m_new = jnp.maximum(m_old, cmax)
            alpha = jnp.exp2(m_old - m_new)
            mb = jnp.broadcast_to(m_new, (MLA_ROWS, bq))
            for r in range(0, bk, MLA_ROWS):
                rows = slice(r, r + MLA_ROWS)
                p_scr[slot, rows, :] = jnp.exp2(s_scr[slot, rows, :] - mb).astype(BF16)
            return m_new, alpha

        cm0 = scores(0, 0)
        acc_scr[...] = jnp.zeros(acc_scr.shape, F32)
        p_scr[1] = jnp.zeros(p_scr.shape[1:], BF16)

        def group(t, carry, scores=scores, values=values, softmax=softmax):
            m, alpha, cm = carry
            for u in range(unroll):
                j = unroll * t + u
                slot = u % 2
                cm_next = scores(jnp.minimum(j + 1, nkv - 1), 1 - slot)
                values(jnp.maximum(j - 1, 0), 1 - slot, alpha)
                m, alpha = softmax(slot, m, cm)
                cm = cm_next
            return m, alpha, cm

        init = (jnp.full((1, bq), NEG, F32), jnp.ones((1, bq), F32), cm0)
        _, alpha, _ = lax.fori_loop(0, nkv // unroll, group, init)
        values(nkv - 1, 1, alpha)
        pad = jnp.zeros((LANES - DV_ROWS, bq), F32)
        acc = jnp.concatenate([acc_scr[...], pad], axis=0).T
        y = acc[:, :DV] / acc[:, DV:DV + 1]
        cols = slice(hh * DV, (hh + 1) * DV)
        o_ref[:, cols] = (y * szb_ref[:, cols].astype(F32)).astype(BF16)


def _mla_call(qt, k, vt, szb, batch, seq, bq, bk):
    t = k.shape[1]
    nq = seq // bq
    assert seq % (2 * bk) == 0 and bk % MLA_ROWS == 0
    bs = pl.BlockSpec
    return pl.pallas_call(
        functools.partial(_mla_kernel, seq, bk),
        grid=(batch, HB // 2, nq),
        in_specs=[bs((2, LANES, bq), lambda b, hp, i: (hp, 0, b * nq + i)),
                  bs((2, seq, LANES), lambda b, hp, i: (hp, b, 0)),
                  bs((2, DV_ROWS, seq), lambda b, hp, i: (hp, 0, b)),
                  bs((bq, LANES), lambda b, hp, i: (b * nq + i, hp))],
        out_specs=bs((bq, LANES), lambda b, hp, i: (b * nq + i, hp)),
        out_shape=jax.ShapeDtypeStruct((t, WIDTH_B), BF16),
        scratch_shapes=[pltpu.VMEM((2, 2, bk, bq), F32), pltpu.VMEM((2, 2, bk, bq), BF16),
                        pltpu.VMEM((2, DV_ROWS, bq), F32)],
        compiler_params=pltpu.CompilerParams(dimension_semantics=("arbitrary", "arbitrary", "arbitrary"),
                                             vmem_limit_bytes=VMEM_LIMIT),
        name="mla",
    )(qt, k, vt, szb)


def _merge_kernel(x_ref, a_ref, b_ref, gga_ref, ggb_ref, wpa_ref, wpb_ref, wo_ref, fg_ref, o_ref):
    oa = jnp.dot(a_ref[...], wpa_ref[...], preferred_element_type=F32)
    ob = jnp.dot(b_ref[...], wpb_ref[...], preferred_element_type=F32)
    merged = gga_ref[...].astype(F32) * oa + ggb_ref[...].astype(F32) * ob
    y = x_ref[...] + jnp.dot(merged.astype(BF16), wo_ref[...], preferred_element_type=F32)
    o_ref[...] = _rms(y, fg_ref[...])


def _merge_call(x2, a, b, gga, ggb, wpa, wpb, wo, fg, tm):
    t = x2.shape[0]
    bs = pl.BlockSpec
    row = lambda i: (i, 0)
    const = lambda i: (0, 0)
    return pl.pallas_call(
        _merge_kernel, grid=(t // tm,),
        in_specs=[bs((tm, D_MODEL), row), bs((tm, WIDTH_A), row), bs((tm, WIDTH_B), row),
                  bs((tm, D_MODEL), row), bs((tm, D_MODEL), row),
                  bs((WIDTH_A, D_MODEL), const), bs((WIDTH_B, D_MODEL), const), bs((D_MODEL, D_MODEL), const),
                  bs((1, D_MODEL), const)],
        out_specs=bs((tm, D_MODEL), row),
        out_shape=jax.ShapeDtypeStruct((t, D_MODEL), F32),
        compiler_params=pltpu.CompilerParams(dimension_semantics=("arbitrary",), vmem_limit_bytes=VMEM_LIMIT),
        name="merge",
    )(x2, a, b, gga, ggb, wpa, wpb, wo, fg)


def _tiles(seq):
    tm = min(512, seq)
    bq = min(512, seq)
    bk = min(256, seq // 2)
    return tm, bq, bk


def _trunk(x, p):
    batch, seq, _ = x.shape
    tm, bq, bk = _tiles(seq)
    x2 = x.reshape(batch * seq, D_MODEL)
    tables = _rope_tables(seq)
    (qa, kva, sza, szb, gga, ggb, q, k, v) = _proj_call(
        x2, seq, p["ln_g"], p["w1"], p["b_gate"], p["q_norm_g"], p["kv_norm_g"], p["wq"], p["wk"], p["wv"],
        tables, tm)
    a = _window_call(qa, kva, sza, p["bias"], p["sink"], batch, seq)
    b = _mla_call(q, k, v, szb, batch, seq, bq, bk)
    y = _merge_call(x2, a, b, gga, ggb, p["wpa"], p["wpb"], p["wo"], p["final_g"], tm)
    return y.reshape(batch, seq, D_MODEL)


def kernel(x_prompt, x_sample, ln_g, w_in, b_gate, sink_a, q_norm_g, kv_norm_g, w_uq, w_ukv,
           w_proj_a, w_proj_b, w_out, rel_bias, final_g):
    assert ln_g.shape[0] == 1, "one layer"
    w1, wq, wk, wv = _pack_weights(w_in[0], w_uq[0], w_ukv[0])
    p = dict(ln_g=ln_g[0][None, :], w1=w1, b_gate=b_gate[0][None, :], q_norm_g=q_norm_g[0][None, :],
             kv_norm_g=kv_norm_g[0][None, :], wq=wq, wk=wk, wv=wv,
             bias=_window_bias(rel_bias), sink=sink_a[0].astype(F32),
             wpa=w_proj_a[0].astype(BF16), wpb=w_proj_b[0].astype(BF16), wo=w_out[0].astype(BF16),
             final_g=final_g[None, :])
    return (_trunk(x_prompt, p), _trunk(x_sample, p))
```

```python
import functools
import math

import jax
import jax.numpy as jnp
from jax import lax
from jax.experimental import pallas as pl
from jax.experimental.pallas import tpu as pltpu

D_MODEL = 1024
BLK = 128
WINDOW = 128
HA = 8
HKV_A = 2
G_A = HA // HKV_A
DH_A = 64
WIDTH_A = HA * DH_A
NUM_BUCKETS = 32
MAX_DISTANCE = 128
HB = 8
Q_LORA = 256
KV_LORA = 128
NOPE = 64
ROPE = 32
DV = 64
WIDTH_B = HB * DV
ROPE_THETA = 10000.0
EPS = 1e-6
NEG = -1e30
LOG2E = math.log2(math.e)
MLA_ALL_HEADS_SEQ = 2048
MLA_UNROLL = 32
DV_ROWS = 80
MLA_ROWS = 16

LANES = 128
VMEM_LIMIT = 56 * 1024 * 1024

F32 = jnp.float32
BF16 = jnp.bfloat16

_W1_GROUPS = (("qa", WIDTH_A), ("kva", 2 * HKV_A * DH_A), ("za", WIDTH_A),
              ("cq", Q_LORA), ("ckv", KV_LORA), ("kr", LANES), ("zb", WIDTH_B),
              ("ga", D_MODEL), ("gb", D_MODEL))
_W1_OFF = {}
_o = 0
for _n, _w in _W1_GROUPS:
    _W1_OFF[_n] = (_o, _o + _w)
    _o += _w
W1_COLS = _o


def _pad_heads(w, n_heads, width):
    k = w.shape[0]
    w = w.reshape(k, n_heads, width)
    out = jnp.zeros((k, n_heads, LANES), w.dtype)
    out = out.at[:, :, :width].set(w)
    return out.reshape(k, n_heads * LANES)


def _pack_weights(w_in, w_uq, w_ukv):
    idx = [0]
    for s in (WIDTH_A, HKV_A * DH_A, HKV_A * DH_A, WIDTH_A, Q_LORA, KV_LORA, ROPE, WIDTH_B, D_MODEL, D_MODEL):
        idx.append(idx[-1] + s)
    qa, ka, va, za, cq, ckv, kr, zb, ga, gb = [w_in[:, idx[i]:idx[i + 1]] for i in range(10)]
    kr_p = jnp.zeros((D_MODEL, LANES), w_in.dtype).at[:, NOPE:NOPE + ROPE].set(kr)
    w1 = jnp.concatenate([qa, ka, va, za, cq, ckv, kr_p, zb, ga, gb], axis=1).astype(BF16)
    wq = _pad_heads(w_uq, HB, NOPE + ROPE).astype(BF16)
    kvf = w_ukv.reshape(KV_LORA, HB, NOPE + DV)
    wk = _pad_heads(kvf[:, :, :NOPE].reshape(KV_LORA, HB * NOPE), HB, NOPE).astype(BF16)
    wv = _pad_heads(kvf[:, :, NOPE:].reshape(KV_LORA, HB * DV), HB, DV).astype(BF16)
    return w1, wq, wk, wv


def _rope_tables(seq):
    half = ROPE // 2
    inv = jnp.power(ROPE_THETA, -jnp.arange(half, dtype=F32) / half)
    ang = jnp.arange(seq, dtype=F32)[:, None] * inv[None, :]
    cos, sin = jnp.cos(ang), jnp.sin(ang)
    z = jnp.zeros((seq, half), F32)
    pad = jnp.zeros((seq, LANES - NOPE - ROPE), F32)
    ones = jnp.ones((seq, NOPE), F32)
    zn = jnp.zeros((seq, NOPE), F32)
    cosq = jnp.concatenate([ones, cos, cos, pad], axis=1)
    sin_lo = jnp.concatenate([zn, -sin, z, pad], axis=1)
    sin_hi = jnp.concatenate([zn, z, sin, pad], axis=1)
    return cosq, sin_lo, sin_hi


def _window_bias(rel_bias):
    span = 2 * BLK
    rel = jnp.arange(-(span - 1), span, dtype=jnp.int32)
    nb = NUM_BUCKETS // 2
    max_exact = nb // 2
    ret = (rel > 0).astype(jnp.int32) * nb
    n = jnp.abs(rel)
    nf = jnp.maximum(n, 1).astype(F32)
    large = max_exact + (jnp.log(nf / max_exact) / math.log(MAX_DISTANCE / max_exact)
                         * (nb - max_exact)).astype(jnp.int32)
    large = jnp.minimum(large, nb - 1)
    bucket = ret + jnp.where(n < max_exact, n, large)
    line = jnp.where((n <= WINDOW)[:, None], rel_bias.astype(F32)[bucket] * LOG2E, NEG)
    period = 2 * span
    line = jnp.concatenate([line, jnp.zeros((1, HA), F32)], axis=0).T
    skew = jnp.tile(line, (1, BLK))[:, :BLK * (period - 1)].reshape(HA, BLK, period - 1)
    bias = skew[:, :, span - 1 - BLK:span - 1 - BLK + 3 * BLK]
    kj = jnp.arange(3 * BLK, dtype=jnp.int32)[None, None, :]
    variants = []
    for e in range(4):
        gone = ((kj < BLK) & (e % 2 == 1)) | ((kj >= 2 * BLK) & (e // 2 == 1))
        variants.append(jnp.where(gone, NEG, bias))
    return jnp.stack(variants)


def _rms(v, g):
    return v * lax.rsqrt(jnp.mean(v * v, axis=-1, keepdims=True) + EPS) * g


def _sigmoid(v):
    return 1.0 / (1.0 + jnp.exp(-v))


def _rot(v, cosq, sin_lo, sin_hi):
    return v * cosq + pltpu.roll(v, LANES - ROPE // 2, 1) * sin_lo + pltpu.roll(v, ROPE // 2, 1) * sin_hi


def _proj_kernel(x_ref, lng_ref, w1_ref, bg_ref, qg_ref, kvg_ref, wq_ref, wk_ref, wv_ref,
                 cos_ref, slo_ref, shi_ref,
                 qa_ref, kva_ref, sza_ref, szb_ref, gga_ref, ggb_ref, q_ref, k_ref, v_ref):
    x = x_ref[...]
    h = _rms(x, lng_ref[...]).astype(BF16)

    def proj(name):
        a, b = _W1_OFF[name]
        return jnp.dot(h, w1_ref[:, a:b], preferred_element_type=F32)

    qa_ref[...] = (proj("qa") * (DH_A ** -0.5 * LOG2E)).astype(BF16)
    kva_ref[...] = proj("kva").astype(BF16)
    za = proj("za")
    sza_ref[...] = (za * _sigmoid(za)).astype(BF16)
    zb = proj("zb")
    szb_ref[...] = (zb * _sigmoid(zb)).astype(BF16)
    gga_ref[...] = _sigmoid(proj("ga") + bg_ref[:, :D_MODEL]).astype(BF16)
    ggb_ref[...] = _sigmoid(proj("gb") + bg_ref[:, D_MODEL:]).astype(BF16)

    cosq, slo, shi = cos_ref[...], slo_ref[...], shi_ref[...]
    cqn = _rms(proj("cq"), qg_ref[...]).astype(BF16)
    qf = jnp.dot(cqn, wq_ref[...], preferred_element_type=F32)
    scale = (NOPE + ROPE) ** -0.5 * LOG2E
    for hh in range(HB):
        blk = qf[:, hh * LANES:(hh + 1) * LANES]
        q_ref[hh] = (_rot(blk, cosq, slo, shi) * scale).T.astype(BF16)

    ckvn = _rms(proj("ckv"), kvg_ref[...]).astype(BF16)
    kf = jnp.dot(ckvn, wk_ref[...], preferred_element_type=F32)
    vf = jnp.dot(ckvn, wv_ref[...], preferred_element_type=F32)
    krr = _rot(proj("kr"), cosq, slo, shi)
    lane = lax.broadcasted_iota(jnp.int32, (1, LANES), 1)
    ones_col = (lane == DV).astype(F32)
    for hh in range(HB):
        k_ref[hh] = (kf[:, hh * LANES:(hh + 1) * LANES] + krr).astype(BF16)
        v_ref[hh] = (vf[:, hh * LANES:(hh + 1) * LANES] + ones_col).T[:DV_ROWS].astype(BF16)


def _proj_call(x2, seq, ln_g, w1, b_gate, q_norm_g, kv_norm_g, wq, wk, wv, tables, tm):
    t = x2.shape[0]
    nt = t // tm
    per_seq = seq // tm
    row = lambda i: (i, 0)
    const = lambda i: (0, 0)
    tab = lambda i: (i % per_seq, 0)
    head = lambda i: (0, i, 0)
    headt = lambda i: (0, 0, i)
    bs = pl.BlockSpec
    kva = 2 * HKV_A * DH_A
    in_specs = [bs((tm, D_MODEL), row), bs((1, D_MODEL), const), bs((D_MODEL, W1_COLS), const, pipeline_mode=pl.Buffered(1)),
                bs((1, 2 * D_MODEL), const), bs((1, Q_LORA), const), bs((1, KV_LORA), const),
                bs((Q_LORA, HB * LANES), const), bs((KV_LORA, HB * LANES), const),
                bs((KV_LORA, HB * LANES), const),
                bs((tm, LANES), tab), bs((tm, LANES), tab), bs((tm, LANES), tab)]
    out_shape = [jax.ShapeDtypeStruct((t, WIDTH_A), BF16), jax.ShapeDtypeStruct((t, kva), BF16),
                 jax.ShapeDtypeStruct((t, WIDTH_A), BF16), jax.ShapeDtypeStruct((t, WIDTH_B), BF16),
                 jax.ShapeDtypeStruct((t, D_MODEL), BF16), jax.ShapeDtypeStruct((t, D_MODEL), BF16),
                 jax.ShapeDtypeStruct((HB, LANES, t), BF16), jax.ShapeDtypeStruct((HB, t, LANES), BF16),
                 jax.ShapeDtypeStruct((HB, DV_ROWS, t), BF16)]
    out_specs = [bs((tm, WIDTH_A), row), bs((tm, kva), row),
                 bs((tm, WIDTH_A), row), bs((tm, WIDTH_B), row), bs((tm, D_MODEL), row), bs((tm, D_MODEL), row),
                 bs((HB, LANES, tm), headt), bs((HB, tm, LANES), head), bs((HB, DV_ROWS, tm), headt)]
    return pl.pallas_call(
        _proj_kernel, grid=(nt,), in_specs=in_specs, out_specs=out_specs, out_shape=out_shape,
        compiler_params=pltpu.CompilerParams(dimension_semantics=("arbitrary",), vmem_limit_bytes=VMEM_LIMIT),
        name="proj",
    )(x2, ln_g, w1, b_gate, q_norm_g, kv_norm_g, wq, wk, wv, *tables)


def _window_kernel(nb, sink_ref, bias_ref, q_ref, kvp_ref, kvc_ref, kvn_ref, sza_ref, o_ref, s_ref):
    n = pl.program_id(1)
    edge = (n == 0).astype(jnp.int32) + 2 * (n == nb - 1).astype(jnp.int32)
    lane = lax.broadcasted_iota(jnp.int32, (1, LANES), 1)

    kv = jnp.concatenate([kvp_ref[...], kvc_ref[...], kvn_ref[...]], axis=0)
    vw = jnp.concatenate([kv[:, HKV_A * DH_A:], jnp.ones((3 * BLK, LANES), BF16)], axis=1)

    for hh in range(HA):
        g = hh // G_A
        s_ref[hh] = lax.dot_general(q_ref[:, hh * DH_A:(hh + 1) * DH_A], kv[:, g * DH_A:(g + 1) * DH_A],
                                    (((1,), (1,)), ((), ())), preferred_element_type=F32)

    halves = []
    for hh in range(HA):
        s = s_ref[hh] + bias_ref[edge, hh]
        sk = sink_ref[hh] * LOG2E
        m = jnp.maximum(jnp.max(s, axis=-1, keepdims=True), sk)
        p = jnp.exp2(s - m).astype(BF16)
        o = jnp.dot(p, vw, preferred_element_type=F32)
        y_h = o[:, :LANES] / (o[:, LANES:] + jnp.exp2(sk - m))
        halves.append(y_h if (hh // G_A) == (hh % 2) else pltpu.roll(y_h, DH_A, 1))
    outs = [jnp.where(lane < DH_A, halves[2 * i], halves[2 * i + 1]) for i in range(HA // 2)]
    y = jnp.concatenate(outs, axis=1)
    o_ref[...] = (y * sza_ref[...].astype(F32)).astype(BF16)


def _window_call(qa, kva, sza, bias, sink, batch, seq):
    t = qa.shape[0]
    nb = seq // BLK
    bs = pl.BlockSpec
    cur = lambda b, n: (b * nb + n, 0)
    prev = lambda b, n: (b * nb + jnp.maximum(n - 1, 0), 0)
    nxt = lambda b, n: (b * nb + jnp.minimum(n + 1, nb - 1), 0)
    smem = functools.partial(bs, memory_space=pltpu.SMEM)
    kvw = 2 * HKV_A * DH_A
    in_specs = [smem(), bs((4, HA, BLK, 3 * BLK), lambda b, n: (0, 0, 0, 0), pipeline_mode=pl.Buffered(1)),
                bs((BLK, WIDTH_A), cur),
                bs((BLK, kvw), prev), bs((BLK, kvw), cur), bs((BLK, kvw), nxt),
                bs((BLK, WIDTH_A), cur)]
    return pl.pallas_call(
        functools.partial(_window_kernel, nb),
        grid=(batch, nb), in_specs=in_specs, out_specs=bs((BLK, WIDTH_A), cur),
        out_shape=jax.ShapeDtypeStruct((t, WIDTH_A), BF16),
        scratch_shapes=[pltpu.VMEM((HA, BLK, 3 * BLK), F32)],
        compiler_params=pltpu.CompilerParams(dimension_semantics=("arbitrary", "arbitrary"),
                                             vmem_limit_bytes=VMEM_LIMIT),
        name="window",
    )(sink, bias, qa, kva, kva, kva, sza)


def _mla_kernel(seq, bk, qt_ref, k_ref, vt_ref, szb_ref, o_ref, s_all, p_all, acc_all):
    bq = qt_ref.shape[2]
    nkv = seq // bk
    unroll = min(MLA_UNROLL, nkv)
    assert unroll % 2 == 0 and nkv % unroll == 0
    for hh in range(qt_ref.shape[0]):
        s_scr, p_scr, acc_scr = s_all.at[hh], p_all.at[hh], acc_all.at[hh]

        def scores(j, slot, hh=hh, s_scr=s_scr):
            start = pl.multiple_of(j * bk, bk)
            s = jnp.dot(k_ref[hh, pl.ds(start, bk), :], qt_ref[hh], preferred_element_type=F32)
            s_scr[slot] = s
            return jnp.max(s, axis=0, keepdims=True)

        def values(j, slot, alpha, hh=hh, p_scr=p_scr, acc_scr=acc_scr):
            start = pl.multiple_of(j * bk, bk)
            pv = jnp.dot(vt_ref[hh, :, pl.ds(start, bk)], p_scr[slot], preferred_element_type=F32)
            acc_scr[...] = acc_scr[...] * alpha + pv

        def softmax(slot, m_old, cmax, s_scr=s_scr, p_scr=p_scr):
            m_new = jnp.maximum(m_old, cmax)
            alpha = jnp.exp2(m_old - m_new)
            mb = jnp.broadcast_to(m_new, (MLA_ROWS, bq))
            for r in range(0, bk, MLA_ROWS):
                rows = slice(r, r + MLA_ROWS)
                p_scr[slot, rows, :] = jnp.exp2(s_scr[slot, rows, :] - mb).astype(BF16)
            return m_new, alpha

        cm0 = scores(0, 0)
        acc_scr[...] = jnp.zeros(acc_scr.shape, F32)
        p_scr[1] = jnp.zeros(p_scr.shape[1:], BF16)

        def group(t, carry, scores=scores, values=values, softmax=softmax):
            m, alpha, cm = carry
            for u in range(unroll):
                j = unroll * t + u
                slot = u % 2
                cm_next = scores(jnp.minimum(j + 1, nkv - 1), 1 - slot)
                values(jnp.maximum(j - 1, 0), 1 - slot, alpha)
                m, alpha = softmax(slot, m, cm)
                cm = cm_next
            return m, alpha, cm

        init = (jnp.full((1, bq), NEG, F32), jnp.ones((1, bq), F32), cm0)
        _, alpha, _ = lax.fori_loop(0, nkv // unroll, group, init)
        values(nkv - 1, 1, alpha)
        pad = jnp.zeros((LANES - DV_ROWS, bq), F32)
        acc = jnp.concatenate([acc_scr[...], pad], axis=0).T
        y = acc[:, :DV] / acc[:, DV:DV + 1]
        cols = slice(hh * DV, (hh + 1) * DV)
        o_ref[:, cols] = (y * szb_ref[:, cols].astype(F32)).astype(BF16)


def _mla_call(qt, k, vt, szb, batch, seq, bq, bk, hps):
    t = k.shape[1]
    nq = seq // bq
    assert seq % (2 * bk) == 0 and bk % MLA_ROWS == 0 and HB % hps == 0 and hps % 2 == 0
    ow = hps * DV
    bs = pl.BlockSpec
    return pl.pallas_call(
        functools.partial(_mla_kernel, seq, bk),
        grid=(batch, HB // hps, nq),
        in_specs=[bs((hps, LANES, bq), lambda b, hp, i: (hp, 0, b * nq + i)),
                  bs((hps, seq, LANES), lambda b, hp, i: (hp, b, 0)),
                  bs((hps, DV_ROWS, seq), lambda b, hp, i: (hp, 0, b)),
                  bs((bq, ow), lambda b, hp, i: (b * nq + i, hp))],
        out_specs=bs((bq, ow), lambda b, hp, i: (b * nq + i, hp)),
        out_shape=jax.ShapeDtypeStruct((t, WIDTH_B), BF16),
        scratch_shapes=[pltpu.VMEM((hps, 2, bk, bq), F32), pltpu.VMEM((hps, 2, bk, bq), BF16),
                        pltpu.VMEM((hps, DV_ROWS, bq), F32)],
        compiler_params=pltpu.CompilerParams(dimension_semantics=("arbitrary", "arbitrary", "arbitrary"),
                                             vmem_limit_bytes=VMEM_LIMIT),
        name="mla",
    )(qt, k, vt, szb)


def _merge_kernel(x_ref, a_ref, b_ref, gga_ref, ggb_ref, wpa_ref, wpb_ref, wo_ref, fg_ref, o_ref):
    oa = jnp.dot(a_ref[...], wpa_ref[...], preferred_element_type=F32)
    ob = jnp.dot(b_ref[...], wpb_ref[...], preferred_element_type=F32)
    merged = gga_ref[...].astype(F32) * oa + ggb_ref[...].astype(F32) * ob
    y = x_ref[...] + jnp.dot(merged.astype(BF16), wo_ref[...], preferred_element_type=F32)
    o_ref[...] = _rms(y, fg_ref[...])


def _merge_call(x2, a, b, gga, ggb, wpa, wpb, wo, fg, tm):
    t = x2.shape[0]
    bs = pl.BlockSpec
    row = lambda i: (i, 0)
    const = lambda i: (0, 0)
    return pl.pallas_call(
        _merge_kernel, grid=(t // tm,),
        in_specs=[bs((tm, D_MODEL), row), bs((tm, WIDTH_A), row), bs((tm, WIDTH_B), row),
                  bs((tm, D_MODEL), row), bs((tm, D_MODEL), row),
                  bs((WIDTH_A, D_MODEL), const), bs((WIDTH_B, D_MODEL), const), bs((D_MODEL, D_MODEL), const),
                  bs((1, D_MODEL), const)],
        out_specs=bs((tm, D_MODEL), row),
        out_shape=jax.ShapeDtypeStruct((t, D_MODEL), F32),
        compiler_params=pltpu.CompilerParams(dimension_semantics=("arbitrary",), vmem_limit_bytes=VMEM_LIMIT),
        name="merge",
    )(x2, a, b, gga, ggb, wpa, wpb, wo, fg)


def _tiles(seq):
    tm = min(512, seq)
    bq = min(512, seq)
    bk = min(256, seq // 2)
    hps = HB if seq <= MLA_ALL_HEADS_SEQ else 2
    return tm, bq, bk, hps


def _trunk(x, p):
    batch, seq, _ = x.shape
    tm, bq, bk, hps = _tiles(seq)
    x2 = x.reshape(batch * seq, D_MODEL)
    tables = _rope_tables(seq)
    (qa, kva, sza, szb, gga, ggb, q, k, v) = _proj_call(
        x2, seq, p["ln_g"], p["w1"], p["b_gate"], p["q_norm_g"], p["kv_norm_g"], p["wq"], p["wk"], p["wv"],
        tables, tm)
    a = _window_call(qa, kva, sza, p["bias"], p["sink"], batch, seq)
    b = _mla_call(q, k, v, szb, batch, seq, bq, bk, hps)
    y = _merge_call(x2, a, b, gga, ggb, p["wpa"], p["wpb"], p["wo"], p["final_g"], tm)
    return y.reshape(batch, seq, D_MODEL)


def kernel(x_prompt, x_sample, ln_g, w_in, b_gate, sink_a, q_norm_g, kv_norm_g, w_uq, w_ukv,
           w_proj_a, w_proj_b, w_out, rel_bias, final_g):
    assert ln_g.shape[0] == 1, "one layer"
    w1, wq, wk, wv = _pack_weights(w_in[0], w_uq[0], w_ukv[0])
    p = dict(ln_g=ln_g[0][None, :], w1=w1, b_gate=b_gate[0][None, :], q_norm_g=q_norm_g[0][None, :],
             kv_norm_g=kv_norm_g[0][None, :], wq=wq, wk=wk, wv=wv,
             bias=_window_bias(rel_bias), sink=sink_a[0].astype(F32),
             wpa=w_proj_a[0].astype(BF16), wpb=w_proj_b[0].astype(BF16), wo=w_out[0].astype(BF16),
             final_g=final_g[None, :])
    return (_trunk(x_prompt, p), _trunk(x_sample, p))
```

```python
import functools
import math

import jax
import jax.numpy as jnp
from jax import lax
from jax.experimental import pallas as pl
from jax.experimental.pallas import tpu as pltpu

D_MODEL = 1024
BLK = 128
WINDOW = 128
HA = 8
HKV_A = 2
G_A = HA // HKV_A
DH_A = 64
WIDTH_A = HA * DH_A
NUM_BUCKETS = 32
MAX_DISTANCE = 128
HB = 8
Q_LORA = 256
KV_LORA = 128
NOPE = 64
ROPE = 32
DV = 64
WIDTH_B = HB * DV
ROPE_THETA = 10000.0
EPS = 1e-6
NEG = -1e30
LOG2E = math.log2(math.e)
WIN_QB = 4
MLA_ALL_HEADS_SEQ = 2048
MLA_UNROLL = 32
DV_ROWS = 80
MLA_ROWS = 16

LANES = 128
VMEM_LIMIT = 56 * 1024 * 1024

F32 = jnp.float32
BF16 = jnp.bfloat16

_W1_GROUPS = (("qa", WIDTH_A), ("kva", 2 * HKV_A * DH_A), ("za", WIDTH_A),
              ("cq", Q_LORA), ("ckv", KV_LORA), ("kr", LANES), ("zb", WIDTH_B),
              ("ga", D_MODEL), ("gb", D_MODEL))
_W1_OFF = {}
_o = 0
for _n, _w in _W1_GROUPS:
    _W1_OFF[_n] = (_o, _o + _w)
    _o += _w
W1_COLS = _o


def _pad_heads(w, n_heads, width):
    k = w.shape[0]
    w = w.reshape(k, n_heads, width)
    out = jnp.zeros((k, n_heads, LANES), w.dtype)
    out = out.at[:, :, :width].set(w)
    return out.reshape(k, n_heads * LANES)


def _pack_weights(w_in, w_uq, w_ukv):
    idx = [0]
    for s in (WIDTH_A, HKV_A * DH_A, HKV_A * DH_A, WIDTH_A, Q_LORA, KV_LORA, ROPE, WIDTH_B, D_MODEL, D_MODEL):
        idx.append(idx[-1] + s)
    qa, ka, va, za, cq, ckv, kr, zb, ga, gb = [w_in[:, idx[i]:idx[i + 1]] for i in range(10)]
    kr_p = jnp.zeros((D_MODEL, LANES), w_in.dtype).at[:, NOPE:NOPE + ROPE].set(kr)
    w1 = jnp.concatenate([qa, ka, va, za, cq, ckv, kr_p, zb, ga, gb], axis=1).astype(BF16)
    wq = _pad_heads(w_uq, HB, NOPE + ROPE).astype(BF16)
    kvf = w_ukv.reshape(KV_LORA, HB, NOPE + DV)
    wk = _pad_heads(kvf[:, :, :NOPE].reshape(KV_LORA, HB * NOPE), HB, NOPE).astype(BF16)
    wv = _pad_heads(kvf[:, :, NOPE:].reshape(KV_LORA, HB * DV), HB, DV).astype(BF16)
    return w1, wq, wk, wv


def _rope_tables(seq):
    half = ROPE // 2
    inv = jnp.power(ROPE_THETA, -jnp.arange(half, dtype=F32) / half)
    ang = jnp.arange(seq, dtype=F32)[:, None] * inv[None, :]
    cos, sin = jnp.cos(ang), jnp.sin(ang)
    z = jnp.zeros((seq, half), F32)
    pad = jnp.zeros((seq, LANES - NOPE - ROPE), F32)
    ones = jnp.ones((seq, NOPE), F32)
    zn = jnp.zeros((seq, NOPE), F32)
    cosq = jnp.concatenate([ones, cos, cos, pad], axis=1)
    sin_lo = jnp.concatenate([zn, -sin, z, pad], axis=1)
    sin_hi = jnp.concatenate([zn, z, sin, pad], axis=1)
    return cosq, sin_lo, sin_hi


def _window_bias(rel_bias):
    span = 2 * BLK
    rel = jnp.arange(-(span - 1), span, dtype=jnp.int32)
    nb = NUM_BUCKETS // 2
    max_exact = nb // 2
    ret = (rel > 0).astype(jnp.int32) * nb
    n = jnp.abs(rel)
    nf = jnp.maximum(n, 1).astype(F32)
    large = max_exact + (jnp.log(nf / max_exact) / math.log(MAX_DISTANCE / max_exact)
                         * (nb - max_exact)).astype(jnp.int32)
    large = jnp.minimum(large, nb - 1)
    bucket = ret + jnp.where(n < max_exact, n, large)
    line = jnp.where((n <= WINDOW)[:, None], rel_bias.astype(F32)[bucket] * LOG2E, NEG)
    period = 2 * span
    line = jnp.concatenate([line, jnp.zeros((1, HA), F32)], axis=0).T
    skew = jnp.tile(line, (1, BLK))[:, :BLK * (period - 1)].reshape(HA, BLK, period - 1)
    bias = skew[:, :, span - 1 - BLK:span - 1 - BLK + 3 * BLK]
    kj = jnp.arange(3 * BLK, dtype=jnp.int32)[None, None, :]
    variants = []
    for e in range(4):
        gone = ((kj < BLK) & (e % 2 == 1)) | ((kj >= 2 * BLK) & (e // 2 == 1))
        variants.append(jnp.where(gone, NEG, bias))
    return jnp.stack(variants)


def _rms(v, g):
    return v * lax.rsqrt(jnp.mean(v * v, axis=-1, keepdims=True) + EPS) * g


def _sigmoid(v):
    return 1.0 / (1.0 + jnp.exp(-v))


def _rot(v, cosq, sin_lo, sin_hi):
    return v * cosq + pltpu.roll(v, LANES - ROPE // 2, 1) * sin_lo + pltpu.roll(v, ROPE // 2, 1) * sin_hi


def _proj_kernel(x_ref, lng_ref, w1_ref, bg_ref, qg_ref, kvg_ref, wq_ref, wk_ref, wv_ref,
                 cos_ref, slo_ref, shi_ref,
                 qa_ref, kva_ref, sza_ref, szb_ref, gga_ref, ggb_ref, q_ref, k_ref, v_ref):
    x = x_ref[...]
    h = _rms(x, lng_ref[...]).astype(BF16)

    def proj(name):
        a, b = _W1_OFF[name]
        return jnp.dot(h, w1_ref[:, a:b], preferred_element_type=F32)

    qa_ref[...] = (proj("qa") * (DH_A ** -0.5 * LOG2E)).astype(BF16)
    kva_ref[...] = proj("kva").astype(BF16)
    za = proj("za")
    sza_ref[...] = (za * _sigmoid(za)).astype(BF16)
    zb = proj("zb")
    szb_ref[...] = (zb * _sigmoid(zb)).astype(BF16)
    gga_ref[...] = _sigmoid(proj("ga") + bg_ref[:, :D_MODEL]).astype(BF16)
    ggb_ref[...] = _sigmoid(proj("gb") + bg_ref[:, D_MODEL:]).astype(BF16)

    cosq, slo, shi = cos_ref[...], slo_ref[...], shi_ref[...]
    cqn = _rms(proj("cq"), qg_ref[...]).astype(BF16)
    qf = jnp.dot(cqn, wq_ref[...], preferred_element_type=F32)
    scale = (NOPE + ROPE) ** -0.5 * LOG2E
    for hh in range(HB):
        blk = qf[:, hh * LANES:(hh + 1) * LANES]
        q_ref[hh] = (_rot(blk, cosq, slo, shi) * scale).T.astype(BF16)

    ckvn = _rms(proj("ckv"), kvg_ref[...]).astype(BF16)
    kf = jnp.dot(ckvn, wk_ref[...], preferred_element_type=F32)
    vf = jnp.dot(ckvn, wv_ref[...], preferred_element_type=F32)
    krr = _rot(proj("kr"), cosq, slo, shi)
    lane = lax.broadcasted_iota(jnp.int32, (1, LANES), 1)
    ones_col = (lane == DV).astype(F32)
    for hh in range(HB):
        k_ref[hh] = (kf[:, hh * LANES:(hh + 1) * LANES] + krr).astype(BF16)
        v_ref[hh] = (vf[:, hh * LANES:(hh + 1) * LANES] + ones_col).T[:DV_ROWS].astype(BF16)


def _proj_call(x2, seq, ln_g, w1, b_gate, q_norm_g, kv_norm_g, wq, wk, wv, tables, tm):
    t = x2.shape[0]
    nt = t // tm
    per_seq = seq // tm
    row = lambda i: (i, 0)
    const = lambda i: (0, 0)
    tab = lambda i: (i % per_seq, 0)
    head = lambda i: (0, i, 0)
    headt = lambda i: (0, 0, i)
    bs = pl.BlockSpec
    kva = 2 * HKV_A * DH_A
    in_specs = [bs((tm, D_MODEL), row), bs((1, D_MODEL), const), bs((D_MODEL, W1_COLS), const, pipeline_mode=pl.Buffered(1)),
                bs((1, 2 * D_MODEL), const), bs((1, Q_LORA), const), bs((1, KV_LORA), const),
                bs((Q_LORA, HB * LANES), const), bs((KV_LORA, HB * LANES), const),
                bs((KV_LORA, HB * LANES), const),
                bs((tm, LANES), tab), bs((tm, LANES), tab), bs((tm, LANES), tab)]
    out_shape = [jax.ShapeDtypeStruct((t, WIDTH_A), BF16), jax.ShapeDtypeStruct((t, kva), BF16),
                 jax.ShapeDtypeStruct((t, WIDTH_A), BF16), jax.ShapeDtypeStruct((t, WIDTH_B), BF16),
                 jax.ShapeDtypeStruct((t, D_MODEL), BF16), jax.ShapeDtypeStruct((t, D_MODEL), BF16),
                 jax.ShapeDtypeStruct((HB, LANES, t), BF16), jax.ShapeDtypeStruct((HB, t, LANES), BF16),
                 jax.ShapeDtypeStruct((HB, DV_ROWS, t), BF16)]
    out_specs = [bs((tm, WIDTH_A), row), bs((tm, kva), row),
                 bs((tm, WIDTH_A), row), bs((tm, WIDTH_B), row), bs((tm, D_MODEL), row), bs((tm, D_MODEL), row),
                 bs((HB, LANES, tm), headt), bs((HB, tm, LANES), head), bs((HB, DV_ROWS, tm), headt)]
    return pl.pallas_call(
        _proj_kernel, grid=(nt,), in_specs=in_specs, out_specs=out_specs, out_shape=out_shape,
        compiler_params=pltpu.CompilerParams(dimension_semantics=("arbitrary",), vmem_limit_bytes=VMEM_LIMIT),
        name="proj",
    )(x2, ln_g, w1, b_gate, q_norm_g, kv_norm_g, wq, wk, wv, *tables)


def _window_kernel(nb, sink_ref, bias_ref, q_ref, kvp_ref, kvc_ref, kvn_ref, sza_ref, o_ref, s_ref):
    n = pl.program_id(1)
    qb = q_ref.shape[0] // BLK
    lane = lax.broadcasted_iota(jnp.int32, (1, LANES), 1)

    kv = jnp.concatenate([kvp_ref[...], kvc_ref[...], kvn_ref[...]], axis=0)
    vw = jnp.concatenate([kv[:, HKV_A * DH_A:], jnp.ones((kv.shape[0], LANES), BF16)], axis=1)

    for j in range(qb):
        for hh in range(HA):
            g = hh // G_A
            s_ref[j * HA + hh] = lax.dot_general(
                q_ref[j * BLK:(j + 1) * BLK, hh * DH_A:(hh + 1) * DH_A],
                kv[j * BLK:(j + 3) * BLK, g * DH_A:(g + 1) * DH_A],
                (((1,), (1,)), ((), ())), preferred_element_type=F32)

    for j in range(qb):
        blk = n * qb + j
        edge = (blk == 0).astype(jnp.int32) + 2 * (blk == nb - 1).astype(jnp.int32)
        halves = []
        for hh in range(HA):
            s = s_ref[j * HA + hh] + bias_ref[edge, hh]
            sk = sink_ref[hh] * LOG2E
            m = jnp.maximum(jnp.max(s, axis=-1, keepdims=True), sk)
            p = jnp.exp2(s - m).astype(BF16)
            o = jnp.dot(p, vw[j * BLK:(j + 3) * BLK], preferred_element_type=F32)
            y_h = o[:, :LANES] / (o[:, LANES:] + jnp.exp2(sk - m))
            halves.append(y_h if (hh // G_A) == (hh % 2) else pltpu.roll(y_h, DH_A, 1))
        outs = [jnp.where(lane < DH_A, halves[2 * i], halves[2 * i + 1]) for i in range(HA // 2)]
        y = jnp.concatenate(outs, axis=1)
        rows = slice(j * BLK, (j + 1) * BLK)
        o_ref[rows, :] = (y * sza_ref[rows, :].astype(F32)).astype(BF16)


def _window_call(qa, kva, sza, bias, sink, batch, seq):
    t = qa.shape[0]
    nb = seq // BLK
    qb = min(WIN_QB, nb)
    assert nb % qb == 0
    ns = nb // qb
    tq = qb * BLK
    bs = pl.BlockSpec
    cur = lambda b, n: (b * ns + n, 0)
    prev = lambda b, n: (b * nb + jnp.maximum(n * qb - 1, 0), 0)
    nxt = lambda b, n: (b * nb + jnp.minimum((n + 1) * qb, nb - 1), 0)
    smem = functools.partial(bs, memory_space=pltpu.SMEM)
    kvw = 2 * HKV_A * DH_A
    in_specs = [smem(), bs((4, HA, BLK, 3 * BLK), lambda b, n: (0, 0, 0, 0), pipeline_mode=pl.Buffered(1)),
                bs((tq, WIDTH_A), cur),
                bs((BLK, kvw), prev), bs((tq, kvw), cur), bs((BLK, kvw), nxt),
                bs((tq, WIDTH_A), cur)]
    return pl.pallas_call(
        functools.partial(_window_kernel, nb),
        grid=(batch, ns), in_specs=in_specs, out_specs=bs((tq, WIDTH_A), cur),
        out_shape=jax.ShapeDtypeStruct((t, WIDTH_A), BF16),
        scratch_shapes=[pltpu.VMEM((qb * HA, BLK, 3 * BLK), F32)],
        compiler_params=pltpu.CompilerParams(dimension_semantics=("arbitrary", "arbitrary"),
                                             vmem_limit_bytes=VMEM_LIMIT),
        name="window",
    )(sink, bias, qa, kva, kva, kva, sza)


def _mla_kernel(seq, bk, qt_ref, k_ref, vt_ref, szb_ref, o_ref, s_all, p_all, acc_all):
    bq = qt_ref.shape[2]
    nkv = seq // bk
    unroll = min(MLA_UNROLL, nkv)
    assert unroll % 2 == 0 and nkv % unroll == 0
    for hh in range(qt_ref.shape[0]):
        s_scr, p_scr, acc_scr = s_all.at[hh], p_all.at[hh], acc_all.at[hh]

        def scores(j, slot, hh=hh, s_scr=s_scr):
            start = pl.multiple_of(j * bk, bk)
            s = jnp.dot(k_ref[hh, pl.ds(start, bk), :], qt_ref[hh], preferred_element_type=F32)
            s_scr[slot] = s
            return jnp.max(s, axis=0, keepdims=True)

        def values(j, slot, alpha, hh=hh, p_scr=p_scr, acc_scr=acc_scr):
            start = pl.multiple_of(j * bk, bk)
            pv = jnp.dot(vt_ref[hh, :, pl.ds(start, bk)], p_scr[slot], preferred_element_type=F32)
            acc_scr[...] = acc_scr[...] * alpha + pv

        def softmax(slot, m_old, cmax, s_scr=s_scr, p_scr=p_scr):
            m_new = jnp.maximum(m_old, cmax)
            alpha = jnp.exp2(m_old - m_new)
            mb = jnp.broadcast_to(m_new, (MLA_ROWS, bq))
            for r in range(0, bk, MLA_ROWS):
                rows = slice(r, r + MLA_ROWS)
                p_scr[slot, rows, :] = jnp.exp2(s_scr[slot, rows, :] - mb).astype(BF16)
            return m_new, alpha

        cm0 = scores(0, 0)
        acc_scr[...] = jnp.zeros(acc_scr.shape, F32)
        p_scr[1] = jnp.zeros(p_scr.shape[1:], BF16)

        def group(t, carry, scores=scores, values=values, softmax=softmax):
            m, alpha, cm = carry
            for u in range(unroll):
                j = unroll * t + u
                slot = u % 2
                cm_next = scores(jnp.minimum(j + 1, nkv - 1), 1 - slot)
                values(jnp.maximum(j - 1, 0), 1 - slot, alpha)
                m, alpha = softmax(slot, m, cm)
                cm = cm_next
            return m, alpha, cm

        init = (jnp.full((1, bq), NEG, F32), jnp.ones((1, bq), F32), cm0)
        _, alpha, _ = lax.fori_loop(0, nkv // unroll, group, init)
        values(nkv - 1, 1, alpha)
        pad = jnp.zeros((LANES - DV_ROWS, bq), F32)
        acc = jnp.concatenate([acc_scr[...], pad], axis=0).T
        y = acc[:, :DV] / acc[:, DV:DV + 1]
        cols = slice(hh * DV, (hh + 1) * DV)
        o_ref[:, cols] = (y * szb_ref[:, cols].astype(F32)).astype(BF16)


def _mla_call(qt, k, vt, szb, batch, seq, bq, bk, hps):
    t = k.shape[1]
    nq = seq // bq
    assert seq % (2 * bk) == 0 and bk % MLA_ROWS == 0 and HB % hps == 0 and hps % 2 == 0
    ow = hps * DV
    bs = pl.BlockSpec
    return pl.pallas_call(
        functools.partial(_mla_kernel, seq, bk),
        grid=(batch, HB // hps, nq),
        in_specs=[bs((hps, LANES, bq), lambda b, hp, i: (hp, 0, b * nq + i)),
                  bs((hps, seq, LANES), lambda b, hp, i: (hp, b, 0)),
                  bs((hps, DV_ROWS, seq), lambda b, hp, i: (hp, 0, b)),
                  bs((bq, ow), lambda b, hp, i: (b * nq + i, hp))],
        out_specs=bs((bq, ow), lambda b, hp, i: (b * nq + i, hp)),
        out_shape=jax.ShapeDtypeStruct((t, WIDTH_B), BF16),
        scratch_shapes=[pltpu.VMEM((hps, 2, bk, bq), F32), pltpu.VMEM((hps, 2, bk, bq), BF16),
                        pltpu.VMEM((hps, DV_ROWS, bq), F32)],
        compiler_params=pltpu.CompilerParams(dimension_semantics=("arbitrary", "arbitrary", "arbitrary"),
                                             vmem_limit_bytes=VMEM_LIMIT),
        name="mla",
    )(qt, k, vt, szb)


def _merge_kernel(x_ref, a_ref, b_ref, gga_ref, ggb_ref, wpa_ref, wpb_ref, wo_ref, fg_ref, o_ref):
    oa = jnp.dot(a_ref[...], wpa_ref[...], preferred_element_type=F32)
    ob = jnp.dot(b_ref[...], wpb_ref[...], preferred_element_type=F32)
    merged = gga_ref[...].astype(F32) * oa + ggb_ref[...].astype(F32) * ob
    y = x_ref[...] + jnp.dot(merged.astype(BF16), wo_ref[...], preferred_element_type=F32)
    o_ref[...] = _rms(y, fg_ref[...])


def _merge_call(x2, a, b, gga, ggb, wpa, wpb, wo, fg, tm):
    t = x2.shape[0]
    bs = pl.BlockSpec
    row = lambda i: (i, 0)
    const = lambda i: (0, 0)
    return pl.pallas_call(
        _merge_kernel, grid=(t // tm,),
        in_specs=[bs((tm, D_MODEL), row), bs((tm, WIDTH_A), row), bs((tm, WIDTH_B), row),
                  bs((tm, D_MODEL), row), bs((tm, D_MODEL), row),
                  bs((WIDTH_A, D_MODEL), const), bs((WIDTH_B, D_MODEL), const), bs((D_MODEL, D_MODEL), const),
                  bs((1, D_MODEL), const)],
        out_specs=bs((tm, D_MODEL), row),
        out_shape=jax.ShapeDtypeStruct((t, D_MODEL), F32),
        compiler_params=pltpu.CompilerParams(dimension_semantics=("arbitrary",), vmem_limit_bytes=VMEM_LIMIT),
        name="merge",
    )(x2, a, b, gga, ggb, wpa, wpb, wo, fg)


def _tiles(seq):
    tm = min(512, seq)
    bq = min(512, seq)
    bk = min(256, seq // 2)
    hps = HB if seq <= MLA_ALL_HEADS_SEQ else 2
    return tm, bq, bk, hps


def _trunk(x, p):
    batch, seq, _ = x.shape
    tm, bq, bk, hps = _tiles(seq)
    x2 = x.reshape(batch * seq, D_MODEL)
    tables = _rope_tables(seq)
    (qa, kva, sza, szb, gga, ggb, q, k, v) = _proj_call(
        x2, seq, p["ln_g"], p["w1"], p["b_gate"], p["q_norm_g"], p["kv_norm_g"], p["wq"], p["wk"], p["wv"],
        tables, tm)
    a = _window_call(qa, kva, sza, p["bias"], p["sink"], batch, seq)
    b = _mla_call(q, k, v, szb, batch, seq, bq, bk, hps)
    y = _merge_call(x2, a, b, gga, ggb, p["wpa"], p["wpb"], p["wo"], p["final_g"], tm)
    return y.reshape(batch, seq, D_MODEL)


def kernel(x_prompt, x_sample, ln_g, w_in, b_gate, sink_a, q_norm_g, kv_norm_g, w_uq, w_ukv,
           w_proj_a, w_proj_b, w_out, rel_bias, final_g):
    assert ln_g.shape[0] == 1, "one layer"
    w1, wq, wk, wv = _pack_weights(w_in[0], w_uq[0], w_ukv[0])
    p = dict(ln_g=ln_g[0][None, :], w1=w1, b_gate=b_gate[0][None, :], q_norm_g=q_norm_g[0][None, :],
             kv_norm_g=kv_norm_g[0][None, :], wq=wq, wk=wk, wv=wv,
             bias=_window_bias(rel_bias), sink=sink_a[0].astype(F32),
             wpa=w_proj_a[0].astype(BF16), wpb=w_proj_b[0].astype(BF16), wo=w_out[0].astype(BF16),
             final_g=final_g[None, :])
    return (_trunk(x_prompt, p), _trunk(x_sample, p))
```

```python
import functools
import math

import jax
import jax.numpy as jnp
from jax import lax
from jax.experimental import pallas as pl
from jax.experimental.pallas import tpu as pltpu

D_MODEL = 1024
BLK = 128
WINDOW = 128
HA = 8
HKV_A = 2
G_A = HA // HKV_A
DH_A = 64
WIDTH_A = HA * DH_A
NUM_BUCKETS = 32
MAX_DISTANCE = 128
HB = 8
Q_LORA = 256
KV_LORA = 128
NOPE = 64
ROPE = 32
DV = 64
WIDTH_B = HB * DV
ROPE_THETA = 10000.0
EPS = 1e-6
NEG = -1e30
LOG2E = math.log2(math.e)
WIN_QB = 4
MLA_ALL_HEADS_SEQ = 2048
MLA_UNROLL = 64
DV_ROWS = 80
MLA_ROWS = 16

LANES = 128
VMEM_LIMIT = 56 * 1024 * 1024

F32 = jnp.float32
BF16 = jnp.bfloat16

_W1_GROUPS = (("qa", WIDTH_A), ("kva", 2 * HKV_A * DH_A), ("za", WIDTH_A),
              ("cq", Q_LORA), ("ckv", KV_LORA), ("kr", LANES), ("zb", WIDTH_B),
              ("ga", D_MODEL), ("gb", D_MODEL))
_W1_OFF = {}
_o = 0
for _n, _w in _W1_GROUPS:
    _W1_OFF[_n] = (_o, _o + _w)
    _o += _w
W1_COLS = _o


def _pad_heads(w, n_heads, width):
    k = w.shape[0]
    w = w.reshape(k, n_heads, width)
    out = jnp.zeros((k, n_heads, LANES), w.dtype)
    out = out.at[:, :, :width].set(w)
    return out.reshape(k, n_heads * LANES)


def _pack_weights(w_in, w_uq, w_ukv):
    idx = [0]
    for s in (WIDTH_A, HKV_A * DH_A, HKV_A * DH_A, WIDTH_A, Q_LORA, KV_LORA, ROPE, WIDTH_B, D_MODEL, D_MODEL):
        idx.append(idx[-1] + s)
    qa, ka, va, za, cq, ckv, kr, zb, ga, gb = [w_in[:, idx[i]:idx[i + 1]] for i in range(10)]
    kr_p = jnp.zeros((D_MODEL, LANES), w_in.dtype).at[:, NOPE:NOPE + ROPE].set(kr)
    w1 = jnp.concatenate([qa, ka, va, za, cq, ckv, kr_p, zb, ga, gb], axis=1).astype(BF16)
    wq = _pad_heads(w_uq, HB, NOPE + ROPE).astype(BF16)
    kvf = w_ukv.reshape(KV_LORA, HB, NOPE + DV)
    wk = _pad_heads(kvf[:, :, :NOPE].reshape(KV_LORA, HB * NOPE), HB, NOPE).astype(BF16)
    wv = _pad_heads(kvf[:, :, NOPE:].reshape(KV_LORA, HB * DV), HB, DV).astype(BF16)
    return w1, wq, wk, wv


def _rope_tables(seq):
    half = ROPE // 2
    inv = jnp.power(ROPE_THETA, -jnp.arange(half, dtype=F32) / half)
    ang = jnp.arange(seq, dtype=F32)[:, None] * inv[None, :]
    cos, sin = jnp.cos(ang), jnp.sin(ang)
    z = jnp.zeros((seq, half), F32)
    pad = jnp.zeros((seq, LANES - NOPE - ROPE), F32)
    ones = jnp.ones((seq, NOPE), F32)
    zn = jnp.zeros((seq, NOPE), F32)
    cosq = jnp.concatenate([ones, cos, cos, pad], axis=1)
    sin_lo = jnp.concatenate([zn, -sin, z, pad], axis=1)
    sin_hi = jnp.concatenate([zn, z, sin, pad], axis=1)
    return cosq, sin_lo, sin_hi


def _window_bias(rel_bias):
    span = 2 * BLK
    rel = jnp.arange(-(span - 1), span, dtype=jnp.int32)
    nb = NUM_BUCKETS // 2
    max_exact = nb // 2
    ret = (rel > 0).astype(jnp.int32) * nb
    n = jnp.abs(rel)
    nf = jnp.maximum(n, 1).astype(F32)
    large = max_exact + (jnp.log(nf / max_exact) / math.log(MAX_DISTANCE / max_exact)
                         * (nb - max_exact)).astype(jnp.int32)
    large = jnp.minimum(large, nb - 1)
    bucket = ret + jnp.where(n < max_exact, n, large)
    line = jnp.where((n <= WINDOW)[:, None], rel_bias.astype(F32)[bucket] * LOG2E, NEG)
    period = 2 * span
    line = jnp.concatenate([line, jnp.zeros((1, HA), F32)], axis=0).T
    skew = jnp.tile(line, (1, BLK))[:, :BLK * (period - 1)].reshape(HA, BLK, period - 1)
    bias = skew[:, :, span - 1 - BLK:span - 1 - BLK + 3 * BLK]
    kj = jnp.arange(3 * BLK, dtype=jnp.int32)[None, None, :]
    variants = []
    for e in range(4):
        gone = ((kj < BLK) & (e % 2 == 1)) | ((kj >= 2 * BLK) & (e // 2 == 1))
        variants.append(jnp.where(gone, NEG, bias))
    return jnp.stack(variants)


def _rms(v, g):
    return v * lax.rsqrt(jnp.mean(v * v, axis=-1, keepdims=True) + EPS) * g


def _sigmoid(v):
    return 1.0 / (1.0 + jnp.exp(-v))


def _rot(v, cosq, sin_lo, sin_hi):
    return v * cosq + pltpu.roll(v, LANES - ROPE // 2, 1) * sin_lo + pltpu.roll(v, ROPE // 2, 1) * sin_hi


def _proj_kernel(x_ref, lng_ref, w1_ref, bg_ref, qg_ref, kvg_ref, wq_ref, wk_ref, wv_ref,
                 cos_ref, slo_ref, shi_ref,
                 qa_ref, kva_ref, sza_ref, szb_ref, gga_ref, ggb_ref, q_ref, k_ref, v_ref):
    x = x_ref[...]
    h = _rms(x, lng_ref[...]).astype(BF16)

    def proj(name):
        a, b = _W1_OFF[name]
        return jnp.dot(h, w1_ref[:, a:b], preferred_element_type=F32)

    qa_ref[...] = (proj("qa") * (DH_A ** -0.5 * LOG2E)).astype(BF16)
    kva_ref[...] = proj("kva").astype(BF16)
    za = proj("za")
    sza_ref[...] = (za * _sigmoid(za)).astype(BF16)
    zb = proj("zb")
    szb_ref[...] = (zb * _sigmoid(zb)).astype(BF16)
    gga_ref[...] = _sigmoid(proj("ga") + bg_ref[:, :D_MODEL]).astype(BF16)
    ggb_ref[...] = _sigmoid(proj("gb") + bg_ref[:, D_MODEL:]).astype(BF16)

    cosq, slo, shi = cos_ref[...], slo_ref[...], shi_ref[...]
    cqn = _rms(proj("cq"), qg_ref[...]).astype(BF16)
    qf = jnp.dot(cqn, wq_ref[...], preferred_element_type=F32)
    scale = (NOPE + ROPE) ** -0.5 * LOG2E
    for hh in range(HB):
        blk = qf[:, hh * LANES:(hh + 1) * LANES]
        q_ref[hh] = (_rot(blk, cosq, slo, shi) * scale).T.astype(BF16)

    ckvn = _rms(proj("ckv"), kvg_ref[...]).astype(BF16)
    kf = jnp.dot(ckvn, wk_ref[...], preferred_element_type=F32)
    vf = jnp.dot(ckvn, wv_ref[...], preferred_element_type=F32)
    krr = _rot(proj("kr"), cosq, slo, shi)
    lane = lax.broadcasted_iota(jnp.int32, (1, LANES), 1)
    ones_col = (lane == DV).astype(F32)
    for hh in range(HB):
        k_ref[hh] = (kf[:, hh * LANES:(hh + 1) * LANES] + krr).astype(BF16)
        v_ref[hh] = (vf[:, hh * LANES:(hh + 1) * LANES] + ones_col).T[:DV_ROWS].astype(BF16)


def _proj_call(x2, seq, ln_g, w1, b_gate, q_norm_g, kv_norm_g, wq, wk, wv, tables, tm):
    t = x2.shape[0]
    nt = t // tm
    per_seq = seq // tm
    row = lambda i: (i, 0)
    const = lambda i: (0, 0)
    tab = lambda i: (i % per_seq, 0)
    head = lambda i: (0, i, 0)
    headt = lambda i: (0, 0, i)
    bs = pl.BlockSpec
    kva = 2 * HKV_A * DH_A
    in_specs = [bs((tm, D_MODEL), row), bs((1, D_MODEL), const), bs((D_MODEL, W1_COLS), const, pipeline_mode=pl.Buffered(1)),
                bs((1, 2 * D_MODEL), const), bs((1, Q_LORA), const), bs((1, KV_LORA), const),
                bs((Q_LORA, HB * LANES), const), bs((KV_LORA, HB * LANES), const),
                bs((KV_LORA, HB * LANES), const),
                bs((tm, LANES), tab), bs((tm, LANES), tab), bs((tm, LANES), tab)]
    out_shape = [jax.ShapeDtypeStruct((t, WIDTH_A), BF16), jax.ShapeDtypeStruct((t, kva), BF16),
                 jax.ShapeDtypeStruct((t, WIDTH_A), BF16), jax.ShapeDtypeStruct((t, WIDTH_B), BF16),
                 jax.ShapeDtypeStruct((t, D_MODEL), BF16), jax.ShapeDtypeStruct((t, D_MODEL), BF16),
                 jax.ShapeDtypeStruct((HB, LANES, t), BF16), jax.ShapeDtypeStruct((HB, t, LANES), BF16),
                 jax.ShapeDtypeStruct((HB, DV_ROWS, t), BF16)]
    out_specs = [bs((tm, WIDTH_A), row), bs((tm, kva), row),
                 bs((tm, WIDTH_A), row), bs((tm, WIDTH_B), row), bs((tm, D_MODEL), row), bs((tm, D_MODEL), row),
                 bs((HB, LANES, tm), headt), bs((HB, tm, LANES), head), bs((HB, DV_ROWS, tm), headt)]
    return pl.pallas_call(
        _proj_kernel, grid=(nt,), in_specs=in_specs, out_specs=out_specs, out_shape=out_shape,
        compiler_params=pltpu.CompilerParams(dimension_semantics=("arbitrary",), vmem_limit_bytes=VMEM_LIMIT),
        name="proj",
    )(x2, ln_g, w1, b_gate, q_norm_g, kv_norm_g, wq, wk, wv, *tables)


def _window_kernel(nb, sink_ref, bias_ref, q_ref, kvp_ref, kvc_ref, kvn_ref, sza_ref, o_ref, s_ref):
    n = pl.program_id(1)
    qb = q_ref.shape[0] // BLK
    lane = lax.broadcasted_iota(jnp.int32, (1, LANES), 1)

    kv = jnp.concatenate([kvp_ref[...], kvc_ref[...], kvn_ref[...]], axis=0)
    vw = jnp.concatenate([kv[:, HKV_A * DH_A:], jnp.ones((kv.shape[0], LANES), BF16)], axis=1)

    for j in range(qb):
        for hh in range(HA):
            g = hh // G_A
            s_ref[j * HA + hh] = lax.dot_general(
                q_ref[j * BLK:(j + 1) * BLK, hh * DH_A:(hh + 1) * DH_A],
                kv[j * BLK:(j + 3) * BLK, g * DH_A:(g + 1) * DH_A],
                (((1,), (1,)), ((), ())), preferred_element_type=F32)

    for j in range(qb):
        blk = n * qb + j
        edge = (blk == 0).astype(jnp.int32) + 2 * (blk == nb - 1).astype(jnp.int32)
        halves = []
        for hh in range(HA):
            s = s_ref[j * HA + hh] + bias_ref[edge, hh]
            sk = sink_ref[hh] * LOG2E
            m = jnp.maximum(jnp.max(s, axis=-1, keepdims=True), sk)
            p = jnp.exp2(s - m).astype(BF16)
            o = jnp.dot(p, vw[j * BLK:(j + 3) * BLK], preferred_element_type=F32)
            y_h = o[:, :LANES] / (o[:, LANES:] + jnp.exp2(sk - m))
            halves.append(y_h if (hh // G_A) == (hh % 2) else pltpu.roll(y_h, DH_A, 1))
        outs = [jnp.where(lane < DH_A, halves[2 * i], halves[2 * i + 1]) for i in range(HA // 2)]
        y = jnp.concatenate(outs, axis=1)
        rows = slice(j * BLK, (j + 1) * BLK)
        o_ref[rows, :] = (y * sza_ref[rows, :].astype(F32)).astype(BF16)


def _window_call(qa, kva, sza, bias, sink, batch, seq):
    t = qa.shape[0]
    nb = seq // BLK
    qb = min(WIN_QB, nb)
    assert nb % qb == 0
    ns = nb // qb
    tq = qb * BLK
    bs = pl.BlockSpec
    cur = lambda b, n: (b * ns + n, 0)
    prev = lambda b, n: (b * nb + jnp.maximum(n * qb - 1, 0), 0)
    nxt = lambda b, n: (b * nb + jnp.minimum((n + 1) * qb, nb - 1), 0)
    smem = functools.partial(bs, memory_space=pltpu.SMEM)
    kvw = 2 * HKV_A * DH_A
    in_specs = [smem(), bs((4, HA, BLK, 3 * BLK), lambda b, n: (0, 0, 0, 0), pipeline_mode=pl.Buffered(1)),
                bs((tq, WIDTH_A), cur),
                bs((BLK, kvw), prev), bs((tq, kvw), cur), bs((BLK, kvw), nxt),
                bs((tq, WIDTH_A), cur)]
    return pl.pallas_call(
        functools.partial(_window_kernel, nb),
        grid=(batch, ns), in_specs=in_specs, out_specs=bs((tq, WIDTH_A), cur),
        out_shape=jax.ShapeDtypeStruct((t, WIDTH_A), BF16),
        scratch_shapes=[pltpu.VMEM((qb * HA, BLK, 3 * BLK), F32)],
        compiler_params=pltpu.CompilerParams(dimension_semantics=("arbitrary", "arbitrary"),
                                             vmem_limit_bytes=VMEM_LIMIT),
        name="window",
    )(sink, bias, qa, kva, kva, kva, sza)


def _mla_kernel(seq, bk, qt_ref, k_ref, vt_ref, szb_ref, o_ref, s_all, p_all, acc_all):
    bq = qt_ref.shape[2]
    nkv = seq // bk
    unroll = min(MLA_UNROLL, nkv)
    assert unroll % 2 == 0 and nkv % unroll == 0
    n_heads = qt_ref.shape[0]

    def scores(hh, j, slot):
        start = pl.multiple_of(j * bk, bk)
        s = jnp.dot(k_ref[hh, pl.ds(start, bk), :], qt_ref[hh], preferred_element_type=F32)
        s_all[hh, slot] = s
        return jnp.max(s, axis=0, keepdims=True)

    def values(hh, j, slot, alpha):
        start = pl.multiple_of(j * bk, bk)
        pv = jnp.dot(vt_ref[hh, :, pl.ds(start, bk)], p_all[hh, slot], preferred_element_type=F32)
        acc_all[hh] = acc_all[hh] * alpha + pv

    def softmax(hh, slot, m_old, cmax):
        m_new = jnp.maximum(m_old, cmax)
        alpha = jnp.exp2(m_old - m_new)
        mb = jnp.broadcast_to(m_new, (MLA_ROWS, bq))
        for r in range(0, bk, MLA_ROWS):
            rows = slice(r, r + MLA_ROWS)
            p_all[hh, slot, rows, :] = jnp.exp2(s_all[hh, slot, rows, :] - mb).astype(BF16)
        return m_new, alpha

    def prologue(hh):
        cm0 = scores(hh, 0, 0)
        acc_all[hh] = jnp.zeros(acc_all.shape[1:], F32)
        p_all[hh, 1] = jnp.zeros(p_all.shape[2:], BF16)
        return cm0

    def epilogue(hh, alpha):
        values(hh, nkv - 1, 1, alpha)
        pad = jnp.zeros((LANES - DV_ROWS, bq), F32)
        acc = jnp.concatenate([acc_all[hh], pad], axis=0).T
        y = acc[:, :DV] / acc[:, DV:DV + 1]
        cols = slice(hh * DV, (hh + 1) * DV)
        o_ref[:, cols] = (y * szb_ref[:, cols].astype(F32)).astype(BF16)

    cm0 = prologue(0)
    for hh in range(n_heads):
        def group(t, carry, hh=hh):
            m, alpha, cm = carry
            for u in range(unroll):
                j = unroll * t + u
                slot = u % 2
                cm_next = scores(hh, jnp.minimum(j + 1, nkv - 1), 1 - slot)
                values(hh, jnp.maximum(j - 1, 0), 1 - slot, alpha)
                m, alpha = softmax(hh, slot, m, cm)
                cm = cm_next
            return m, alpha, cm

        init = (jnp.full((1, bq), NEG, F32), jnp.ones((1, bq), F32), cm0)
        _, alpha, _ = lax.fori_loop(0, nkv // unroll, group, init)
        if hh + 1 < n_heads:
            cm0 = prologue(hh + 1)
        epilogue(hh, alpha)


def _mla_call(qt, k, vt, szb, batch, seq, bq, bk, hps):
    t = k.shape[1]
    nq = seq // bq
    assert seq % (2 * bk) == 0 and bk % MLA_ROWS == 0 and HB % hps == 0 and hps % 2 == 0
    ow = hps * DV
    bs = pl.BlockSpec
    return pl.pallas_call(
        functools.partial(_mla_kernel, seq, bk),
        grid=(batch, HB // hps, nq),
        in_specs=[bs((hps, LANES, bq), lambda b, hp, i: (hp, 0, b * nq + i)),
                  bs((hps, seq, LANES), lambda b, hp, i: (hp, b, 0)),
                  bs((hps, DV_ROWS, seq), lambda b, hp, i: (hp, 0, b)),
                  bs((bq, ow), lambda b, hp, i: (b * nq + i, hp))],
        out_specs=bs((bq, ow), lambda b, hp, i: (b * nq + i, hp)),
        out_shape=jax.ShapeDtypeStruct((t, WIDTH_B), BF16),
        scratch_shapes=[pltpu.VMEM((hps, 2, bk, bq), F32), pltpu.VMEM((hps, 2, bk, bq), BF16),
                        pltpu.VMEM((hps, DV_ROWS, bq), F32)],
        compiler_params=pltpu.CompilerParams(dimension_semantics=("arbitrary", "arbitrary", "arbitrary"),
                                             vmem_limit_bytes=VMEM_LIMIT),
        name="mla",
    )(qt, k, vt, szb)


def _merge_kernel(x_ref, a_ref, b_ref, gga_ref, ggb_ref, wpa_ref, wpb_ref, wo_ref, fg_ref, o_ref):
    oa = jnp.dot(a_ref[...], wpa_ref[...], preferred_element_type=F32)
    ob = jnp.dot(b_ref[...], wpb_ref[...], preferred_element_type=F32)
    merged = gga_ref[...].astype(F32) * oa + ggb_ref[...].astype(F32) * ob
    y = x_ref[...] + jnp.dot(merged.astype(BF16), wo_ref[...], preferred_element_type=F32)
    o_ref[...] = _rms(y, fg_ref[...])


def _merge_call(x2, a, b, gga, ggb, wpa, wpb, wo, fg, tm):
    t = x2.shape[0]
    bs = pl.BlockSpec
    row = lambda i: (i, 0)
    const = lambda i: (0, 0)
    return pl.pallas_call(
        _merge_kernel, grid=(t // tm,),
        in_specs=[bs((tm, D_MODEL), row), bs((tm, WIDTH_A), row), bs((tm, WIDTH_B), row),
                  bs((tm, D_MODEL), row), bs((tm, D_MODEL), row),
                  bs((WIDTH_A, D_MODEL), const), bs((WIDTH_B, D_MODEL), const), bs((D_MODEL, D_MODEL), const),
                  bs((1, D_MODEL), const)],
        out_specs=bs((tm, D_MODEL), row),
        out_shape=jax.ShapeDtypeStruct((t, D_MODEL), F32),
        compiler_params=pltpu.CompilerParams(dimension_semantics=("arbitrary",), vmem_limit_bytes=VMEM_LIMIT),
        name="merge",
    )(x2, a, b, gga, ggb, wpa, wpb, wo, fg)


def _tiles(seq):
    tm = min(512, seq)
    bq = min(512, seq)
    bk = min(256, seq // 2)
    hps = HB if seq <= MLA_ALL_HEADS_SEQ else 2
    return tm, bq, bk, hps


def _trunk(x, p):
    batch, seq, _ = x.shape
    tm, bq, bk, hps = _tiles(seq)
    x2 = x.reshape(batch * seq, D_MODEL)
    tables = _rope_tables(seq)
    (qa, kva, sza, szb, gga, ggb, q, k, v) = _proj_call(
        x2, seq, p["ln_g"], p["w1"], p["b_gate"], p["q_norm_g"], p["kv_norm_g"], p["wq"], p["wk"], p["wv"],
        tables, tm)
    a = _window_call(qa, kva, sza, p["bias"], p["sink"], batch, seq)
    b = _mla_call(q, k, v, szb, batch, seq, bq, bk, hps)
    y = _merge_call(x2, a, b, gga, ggb, p["wpa"], p["wpb"], p["wo"], p["final_g"], tm)
    return y.reshape(batch, seq, D_MODEL)


def kernel(x_prompt, x_sample, ln_g, w_in, b_gate, sink_a, q_norm_g, kv_norm_g, w_uq, w_ukv,
           w_proj_a, w_proj_b, w_out, rel_bias, final_g):
    assert ln_g.shape[0] == 1, "one layer"
    w1, wq, wk, wv = _pack_weights(w_in[0], w_uq[0], w_ukv[0])
    p = dict(ln_g=ln_g[0][None, :], w1=w1, b_gate=b_gate[0][None, :], q_norm_g=q_norm_g[0][None, :],
             kv_norm_g=kv_norm_g[0][None, :], wq=wq, wk=wk, wv=wv,
             bias=_window_bias(rel_bias), sink=sink_a[0].astype(F32),
             wpa=w_proj_a[0].astype(BF16), wpb=w_proj_b[0].astype(BF16), wo=w_out[0].astype(BF16),
             final_g=final_g[None, :])
    return (_trunk(x_prompt, p), _trunk(x_sample, p))
```

```python
import functools
import math

import jax
import jax.numpy as jnp
from jax import lax
from jax.experimental import pallas as pl
from jax.experimental.pallas import tpu as pltpu

D_MODEL = 1024
BLK = 128
WINDOW = 128
HA = 8
HKV_A = 2
G_A = HA // HKV_A
DH_A = 64
WIDTH_A = HA * DH_A
NUM_BUCKETS = 32
MAX_DISTANCE = 128
HB = 8
Q_LORA = 256
KV_LORA = 128
NOPE = 64
ROPE = 32
DV = 64
WIDTH_B = HB * DV
ROPE_THETA = 10000.0
EPS = 1e-6
NEG = -1e30
LOG2E = math.log2(math.e)
TOKEN_TILE = 512
MLA_Q_TILE = 512
MLA_K_TILE = 256
WIN_QB = 4
MLA_ALL_HEADS_SEQ = 2048
MLA_UNROLL = 64
DV_ROWS = 80
MLA_ROWS = 16

LANES = 128
VMEM_LIMIT = 56 * 1024 * 1024

F32 = jnp.float32
BF16 = jnp.bfloat16

_W1_GROUPS = (("qa", WIDTH_A), ("kva", 2 * HKV_A * DH_A), ("za", WIDTH_A),
              ("cq", Q_LORA), ("ckv", KV_LORA), ("kr", LANES), ("zb", WIDTH_B),
              ("ga", D_MODEL), ("gb", D_MODEL))
_W1_OFF = {}
_o = 0
for _n, _w in _W1_GROUPS:
    _W1_OFF[_n] = (_o, _o + _w)
    _o += _w
W1_COLS = _o


def _pad_heads(w, n_heads, width):
    k = w.shape[0]
    w = w.reshape(k, n_heads, width)
    out = jnp.zeros((k, n_heads, LANES), w.dtype)
    out = out.at[:, :, :width].set(w)
    return out.reshape(k, n_heads * LANES)


def _pack_weights(w_in, w_uq, w_ukv):
    idx = [0]
    for s in (WIDTH_A, HKV_A * DH_A, HKV_A * DH_A, WIDTH_A, Q_LORA, KV_LORA, ROPE, WIDTH_B, D_MODEL, D_MODEL):
        idx.append(idx[-1] + s)
    qa, ka, va, za, cq, ckv, kr, zb, ga, gb = [w_in[:, idx[i]:idx[i + 1]] for i in range(10)]
    kr_p = jnp.zeros((D_MODEL, LANES), w_in.dtype).at[:, NOPE:NOPE + ROPE].set(kr)
    w1 = jnp.concatenate([qa, ka, va, za, cq, ckv, kr_p, zb, ga, gb], axis=1).astype(BF16)
    wq = _pad_heads(w_uq, HB, NOPE + ROPE).astype(BF16)
    kvf = w_ukv.reshape(KV_LORA, HB, NOPE + DV)
    wk = _pad_heads(kvf[:, :, :NOPE].reshape(KV_LORA, HB * NOPE), HB, NOPE).astype(BF16)
    wv = _pad_heads(kvf[:, :, NOPE:].reshape(KV_LORA, HB * DV), HB, DV).astype(BF16)
    return w1, wq, wk, wv


def _rope_tables(seq):
    half = ROPE // 2
    inv = jnp.power(ROPE_THETA, -jnp.arange(half, dtype=F32) / half)
    ang = jnp.arange(seq, dtype=F32)[:, None] * inv[None, :]
    cos, sin = jnp.cos(ang), jnp.sin(ang)
    z = jnp.zeros((seq, half), F32)
    pad = jnp.zeros((seq, LANES - NOPE - ROPE), F32)
    ones = jnp.ones((seq, NOPE), F32)
    zn = jnp.zeros((seq, NOPE), F32)
    cosq = jnp.concatenate([ones, cos, cos, pad], axis=1)
    sin_lo = jnp.concatenate([zn, -sin, z, pad], axis=1)
    sin_hi = jnp.concatenate([zn, z, sin, pad], axis=1)
    return cosq, sin_lo, sin_hi


def _window_bias(rel_bias):
    span = 2 * BLK
    rel = jnp.arange(-(span - 1), span, dtype=jnp.int32)
    nb = NUM_BUCKETS // 2
    max_exact = nb // 2
    ret = (rel > 0).astype(jnp.int32) * nb
    n = jnp.abs(rel)
    nf = jnp.maximum(n, 1).astype(F32)
    large = max_exact + (jnp.log(nf / max_exact) / math.log(MAX_DISTANCE / max_exact)
                         * (nb - max_exact)).astype(jnp.int32)
    large = jnp.minimum(large, nb - 1)
    bucket = ret + jnp.where(n < max_exact, n, large)
    line = jnp.where((n <= WINDOW)[:, None], rel_bias.astype(F32)[bucket] * LOG2E, NEG)
    period = 2 * span
    line = jnp.concatenate([line, jnp.zeros((1, HA), F32)], axis=0).T
    skew = jnp.tile(line, (1, BLK))[:, :BLK * (period - 1)].reshape(HA, BLK, period - 1)
    bias = skew[:, :, span - 1 - BLK:span - 1 - BLK + 3 * BLK]
    kj = jnp.arange(3 * BLK, dtype=jnp.int32)[None, None, :]
    variants = []
    for e in range(4):
        gone = ((kj < BLK) & (e % 2 == 1)) | ((kj >= 2 * BLK) & (e // 2 == 1))
        variants.append(jnp.where(gone, NEG, bias))
    return jnp.stack(variants)


def _rms(v, g):
    return v * lax.rsqrt(jnp.mean(v * v, axis=-1, keepdims=True) + EPS) * g


def _sigmoid(v):
    return 1.0 / (1.0 + jnp.exp(-v))


def _rot(v, cosq, sin_lo, sin_hi):
    return v * cosq + pltpu.roll(v, LANES - ROPE // 2, 1) * sin_lo + pltpu.roll(v, ROPE // 2, 1) * sin_hi


def _proj_kernel(x_ref, lng_ref, w1_ref, bg_ref, qg_ref, kvg_ref, wq_ref, wk_ref, wv_ref,
                 cos_ref, slo_ref, shi_ref,
                 qa_ref, kva_ref, sza_ref, szb_ref, gga_ref, ggb_ref, q_ref, k_ref, v_ref):
    x = x_ref[...]
    h = _rms(x, lng_ref[...]).astype(BF16)

    def proj(name):
        a, b = _W1_OFF[name]
        return jnp.dot(h, w1_ref[:, a:b], preferred_element_type=F32)

    qa_ref[...] = (proj("qa") * (DH_A ** -0.5 * LOG2E)).astype(BF16)
    kva_ref[...] = proj("kva").astype(BF16)
    za = proj("za")
    sza_ref[...] = (za * _sigmoid(za)).astype(BF16)
    zb = proj("zb")
    szb_ref[...] = (zb * _sigmoid(zb)).astype(BF16)
    gga_ref[...] = _sigmoid(proj("ga") + bg_ref[:, :D_MODEL]).astype(BF16)
    ggb_ref[...] = _sigmoid(proj("gb") + bg_ref[:, D_MODEL:]).astype(BF16)

    cosq, slo, shi = cos_ref[...], slo_ref[...], shi_ref[...]
    cqn = _rms(proj("cq"), qg_ref[...]).astype(BF16)
    qf = jnp.dot(cqn, wq_ref[...], preferred_element_type=F32)
    scale = (NOPE + ROPE) ** -0.5 * LOG2E
    for hh in range(HB):
        blk = qf[:, hh * LANES:(hh + 1) * LANES]
        q_ref[hh] = (_rot(blk, cosq, slo, shi) * scale).T.astype(BF16)

    ckvn = _rms(proj("ckv"), kvg_ref[...]).astype(BF16)
    kf = jnp.dot(ckvn, wk_ref[...], preferred_element_type=F32)
    vf = jnp.dot(ckvn, wv_ref[...], preferred_element_type=F32)
    krr = _rot(proj("kr"), cosq, slo, shi)
    lane = lax.broadcasted_iota(jnp.int32, (1, LANES), 1)
    ones_col = (lane == DV).astype(F32)
    for hh in range(HB):
        k_ref[hh] = (kf[:, hh * LANES:(hh + 1) * LANES] + krr).astype(BF16)
        v_ref[hh] = (vf[:, hh * LANES:(hh + 1) * LANES] + ones_col).T[:DV_ROWS].astype(BF16)


def _proj_call(x2, seq, ln_g, w1, b_gate, q_norm_g, kv_norm_g, wq, wk, wv, tables, tm):
    t = x2.shape[0]
    nt = t // tm
    per_seq = seq // tm
    row = lambda i: (i, 0)
    const = lambda i: (0, 0)
    tab = lambda i: (i % per_seq, 0)
    head = lambda i: (0, i, 0)
    headt = lambda i: (0, 0, i)
    bs = pl.BlockSpec
    kva = 2 * HKV_A * DH_A
    in_specs = [bs((tm, D_MODEL), row), bs((1, D_MODEL), const), bs((D_MODEL, W1_COLS), const, pipeline_mode=pl.Buffered(1)),
                bs((1, 2 * D_MODEL), const), bs((1, Q_LORA), const), bs((1, KV_LORA), const),
                bs((Q_LORA, HB * LANES), const), bs((KV_LORA, HB * LANES), const),
                bs((KV_LORA, HB * LANES), const),
                bs((tm, LANES), tab), bs((tm, LANES), tab), bs((tm, LANES), tab)]
    out_shape = [jax.ShapeDtypeStruct((t, WIDTH_A), BF16), jax.ShapeDtypeStruct((t, kva), BF16),
                 jax.ShapeDtypeStruct((t, WIDTH_A), BF16), jax.ShapeDtypeStruct((t, WIDTH_B), BF16),
                 jax.ShapeDtypeStruct((t, D_MODEL), BF16), jax.ShapeDtypeStruct((t, D_MODEL), BF16),
                 jax.ShapeDtypeStruct((HB, LANES, t), BF16), jax.ShapeDtypeStruct((HB, t, LANES), BF16),
                 jax.ShapeDtypeStruct((HB, DV_ROWS, t), BF16)]
    out_specs = [bs((tm, WIDTH_A), row), bs((tm, kva), row),
                 bs((tm, WIDTH_A), row), bs((tm, WIDTH_B), row), bs((tm, D_MODEL), row), bs((tm, D_MODEL), row),
                 bs((HB, LANES, tm), headt), bs((HB, tm, LANES), head), bs((HB, DV_ROWS, tm), headt)]
    return pl.pallas_call(
        _proj_kernel, grid=(nt,), in_specs=in_specs, out_specs=out_specs, out_shape=out_shape,
        compiler_params=pltpu.CompilerParams(dimension_semantics=("arbitrary",), vmem_limit_bytes=VMEM_LIMIT),
        name="proj",
    )(x2, ln_g, w1, b_gate, q_norm_g, kv_norm_g, wq, wk, wv, *tables)


def _window_kernel(nb, sink_ref, bias_ref, q_ref, kvp_ref, kvc_ref, kvn_ref, sza_ref, o_ref, s_ref):
    n = pl.program_id(1)
    qb = q_ref.shape[0] // BLK
    lane = lax.broadcasted_iota(jnp.int32, (1, LANES), 1)

    kv = jnp.concatenate([kvp_ref[...], kvc_ref[...], kvn_ref[...]], axis=0)
    vw = jnp.concatenate([kv[:, HKV_A * DH_A:], jnp.ones((kv.shape[0], LANES), BF16)], axis=1)

    for j in range(qb):
        for hh in range(HA):
            g = hh // G_A
            s_ref[j * HA + hh] = lax.dot_general(
                q_ref[j * BLK:(j + 1) * BLK, hh * DH_A:(hh + 1) * DH_A],
                kv[j * BLK:(j + 3) * BLK, g * DH_A:(g + 1) * DH_A],
                (((1,), (1,)), ((), ())), preferred_element_type=F32)

    for j in range(qb):
        blk = n * qb + j
        edge = (blk == 0).astype(jnp.int32) + 2 * (blk == nb - 1).astype(jnp.int32)
        halves = []
        for hh in range(HA):
            s = s_ref[j * HA + hh] + bias_ref[edge, hh]
            sk = sink_ref[hh] * LOG2E
            m = jnp.maximum(jnp.max(s, axis=-1, keepdims=True), sk)
            p = jnp.exp2(s - m).astype(BF16)
            o = jnp.dot(p, vw[j * BLK:(j + 3) * BLK], preferred_element_type=F32)
            y_h = o[:, :LANES] / (o[:, LANES:] + jnp.exp2(sk - m))
            halves.append(y_h if (hh // G_A) == (hh % 2) else pltpu.roll(y_h, DH_A, 1))
        outs = [jnp.where(lane < DH_A, halves[2 * i], halves[2 * i + 1]) for i in range(HA // 2)]
        y = jnp.concatenate(outs, axis=1)
        rows = slice(j * BLK, (j + 1) * BLK)
        o_ref[rows, :] = (y * sza_ref[rows, :].astype(F32)).astype(BF16)


def _window_call(qa, kva, sza, bias, sink, batch, seq):
    t = qa.shape[0]
    nb = seq // BLK
    qb = min(WIN_QB, nb)
    assert nb % qb == 0
    ns = nb // qb
    tq = qb * BLK
    bs = pl.BlockSpec
    cur = lambda b, n: (b * ns + n, 0)
    prev = lambda b, n: (b * nb + jnp.maximum(n * qb - 1, 0), 0)
    nxt = lambda b, n: (b * nb + jnp.minimum((n + 1) * qb, nb - 1), 0)
    smem = functools.partial(bs, memory_space=pltpu.SMEM)
    kvw = 2 * HKV_A * DH_A
    in_specs = [smem(), bs((4, HA, BLK, 3 * BLK), lambda b, n: (0, 0, 0, 0), pipeline_mode=pl.Buffered(1)),
                bs((tq, WIDTH_A), cur),
                bs((BLK, kvw), prev), bs((tq, kvw), cur), bs((BLK, kvw), nxt),
                bs((tq, WIDTH_A), cur)]
    return pl.pallas_call(
        functools.partial(_window_kernel, nb),
        grid=(batch, ns), in_specs=in_specs, out_specs=bs((tq, WIDTH_A), cur),
        out_shape=jax.ShapeDtypeStruct((t, WIDTH_A), BF16),
        scratch_shapes=[pltpu.VMEM((qb * HA, BLK, 3 * BLK), F32)],
        compiler_params=pltpu.CompilerParams(dimension_semantics=("arbitrary", "arbitrary"),
                                             vmem_limit_bytes=VMEM_LIMIT),
        name="window",
    )(sink, bias, qa, kva, kva, kva, sza)


def _mla_kernel(seq, bk, qt_ref, k_ref, vt_ref, szb_ref, o_ref, s_all, p_all, acc_all):
    bq = qt_ref.shape[2]
    nkv = seq // bk
    unroll = min(MLA_UNROLL, nkv)
    assert unroll % 2 == 0 and nkv % unroll == 0
    n_heads = qt_ref.shape[0]

    def scores(hh, j, slot):
        start = pl.multiple_of(j * bk, bk)
        s = jnp.dot(k_ref[hh, pl.ds(start, bk), :], qt_ref[hh], preferred_element_type=F32)
        s_all[hh, slot] = s
        return jnp.max(s, axis=0, keepdims=True)

    def values(hh, j, slot, alpha):
        start = pl.multiple_of(j * bk, bk)
        pv = jnp.dot(vt_ref[hh, :, pl.ds(start, bk)], p_all[hh, slot], preferred_element_type=F32)
        acc_all[hh] = acc_all[hh] * alpha + pv

    def softmax(hh, slot, m_old, cmax):
        m_new = jnp.maximum(m_old, cmax)
        alpha = jnp.exp2(m_old - m_new)
        mb = jnp.broadcast_to(m_new, (MLA_ROWS, bq))
        for r in range(0, bk, MLA_ROWS):
            rows = slice(r, r + MLA_ROWS)
            p_all[hh, slot, rows, :] = jnp.exp2(s_all[hh, slot, rows, :] - mb).astype(BF16)
        return m_new, alpha

    def prologue(hh):
        cm0 = scores(hh, 0, 0)
        acc_all[hh] = jnp.zeros(acc_all.shape[1:], F32)
        p_all[hh, 1] = jnp.zeros(p_all.shape[2:], BF16)
        return cm0

    def epilogue(hh, alpha):
        values(hh, nkv - 1, 1, alpha)
        pad = jnp.zeros((LANES - DV_ROWS, bq), F32)
        acc = jnp.concatenate([acc_all[hh], pad], axis=0).T
        y = acc[:, :DV] / acc[:, DV:DV + 1]
        cols = slice(hh * DV, (hh + 1) * DV)
        o_ref[:, cols] = (y * szb_ref[:, cols].astype(F32)).astype(BF16)

    cm0 = prologue(0)
    for hh in range(n_heads):
        def group(t, carry, hh=hh):
            m, alpha, cm = carry
            for u in range(unroll):
                j = unroll * t + u
                slot = u % 2
                cm_next = scores(hh, jnp.minimum(j + 1, nkv - 1), 1 - slot)
                values(hh, jnp.maximum(j - 1, 0), 1 - slot, alpha)
                m, alpha = softmax(hh, slot, m, cm)
                cm = cm_next
            return m, alpha, cm

        init = (jnp.full((1, bq), NEG, F32), jnp.ones((1, bq), F32), cm0)
        _, alpha, _ = lax.fori_loop(0, nkv // unroll, group, init)
        if hh + 1 < n_heads:
            cm0 = prologue(hh + 1)
        epilogue(hh, alpha)


def _mla_call(qt, k, vt, szb, batch, seq, bq, bk, hps):
    t = k.shape[1]
    nq = seq // bq
    assert seq % (2 * bk) == 0 and bk % MLA_ROWS == 0 and HB % hps == 0 and hps % 2 == 0
    ow = hps * DV
    bs = pl.BlockSpec
    return pl.pallas_call(
        functools.partial(_mla_kernel, seq, bk),
        grid=(batch, HB // hps, nq),
        in_specs=[bs((hps, LANES, bq), lambda b, hp, i: (hp, 0, b * nq + i)),
                  bs((hps, seq, LANES), lambda b, hp, i: (hp, b, 0)),
                  bs((hps, DV_ROWS, seq), lambda b, hp, i: (hp, 0, b)),
                  bs((bq, ow), lambda b, hp, i: (b * nq + i, hp))],
        out_specs=bs((bq, ow), lambda b, hp, i: (b * nq + i, hp)),
        out_shape=jax.ShapeDtypeStruct((t, WIDTH_B), BF16),
        scratch_shapes=[pltpu.VMEM((hps, 2, bk, bq), F32), pltpu.VMEM((hps, 2, bk, bq), BF16),
                        pltpu.VMEM((hps, DV_ROWS, bq), F32)],
        compiler_params=pltpu.CompilerParams(dimension_semantics=("arbitrary", "arbitrary", "arbitrary"),
                                             vmem_limit_bytes=VMEM_LIMIT),
        name="mla",
    )(qt, k, vt, szb)


def _merge_kernel(x_ref, a_ref, b_ref, gga_ref, ggb_ref, wpa_ref, wpb_ref, wo_ref, fg_ref, o_ref):
    oa = jnp.dot(a_ref[...], wpa_ref[...], preferred_element_type=F32)
    ob = jnp.dot(b_ref[...], wpb_ref[...], preferred_element_type=F32)
    merged = gga_ref[...].astype(F32) * oa + ggb_ref[...].astype(F32) * ob
    y = x_ref[...] + jnp.dot(merged.astype(BF16), wo_ref[...], preferred_element_type=F32)
    o_ref[...] = _rms(y, fg_ref[...])


def _merge_call(x2, a, b, gga, ggb, wpa, wpb, wo, fg, tm):
    t = x2.shape[0]
    bs = pl.BlockSpec
    row = lambda i: (i, 0)
    const = lambda i: (0, 0)
    return pl.pallas_call(
        _merge_kernel, grid=(t // tm,),
        in_specs=[bs((tm, D_MODEL), row), bs((tm, WIDTH_A), row), bs((tm, WIDTH_B), row),
                  bs((tm, D_MODEL), row), bs((tm, D_MODEL), row),
                  bs((WIDTH_A, D_MODEL), const), bs((WIDTH_B, D_MODEL), const), bs((D_MODEL, D_MODEL), const),
                  bs((1, D_MODEL), const)],
        out_specs=bs((tm, D_MODEL), row),
        out_shape=jax.ShapeDtypeStruct((t, D_MODEL), F32),
        compiler_params=pltpu.CompilerParams(dimension_semantics=("arbitrary",), vmem_limit_bytes=VMEM_LIMIT),
        name="merge",
    )(x2, a, b, gga, ggb, wpa, wpb, wo, fg)


def _tiles(seq):
    tm = min(TOKEN_TILE, seq)
    bq = min(MLA_Q_TILE, seq)
    bk = min(MLA_K_TILE, seq // 2)
    hps = HB if seq <= MLA_ALL_HEADS_SEQ else 2
    return tm, bq, bk, hps


def _trunk(x, p):
    batch, seq, _ = x.shape
    tm, bq, bk, hps = _tiles(seq)
    x2 = x.reshape(batch * seq, D_MODEL)
    tables = _rope_tables(seq)
    (qa, kva, sza, szb, gga, ggb, q, k, v) = _proj_call(
        x2, seq, p["ln_g"], p["w1"], p["b_gate"], p["q_norm_g"], p["kv_norm_g"], p["wq"], p["wk"], p["wv"],
        tables, tm)
    a = _window_call(qa, kva, sza, p["bias"], p["sink"], batch, seq)
    b = _mla_call(q, k, v, szb, batch, seq, bq, bk, hps)
    y = _merge_call(x2, a, b, gga, ggb, p["wpa"], p["wpb"], p["wo"], p["final_g"], tm)
    return y.reshape(batch, seq, D_MODEL)


def kernel(x_prompt, x_sample, ln_g, w_in, b_gate, sink_a, q_norm_g, kv_norm_g, w_uq, w_ukv,
           w_proj_a, w_proj_b, w_out, rel_bias, final_g):
    assert ln_g.shape[0] == 1, "one layer"
    w1, wq, wk, wv = _pack_weights(w_in[0], w_uq[0], w_ukv[0])
    p = dict(ln_g=ln_g[0][None, :], w1=w1, b_gate=b_gate[0][None, :], q_norm_g=q_norm_g[0][None, :],
             kv_norm_g=kv_norm_g[0][None, :], wq=wq, wk=wk, wv=wv,
             bias=_window_bias(rel_bias), sink=sink_a[0].astype(F32),
             wpa=w_proj_a[0].astype(BF16), wpb=w_proj_b[0].astype(BF16), wo=w_out[0].astype(BF16),
             final_g=final_g[None, :])
    return (_trunk(x_prompt, p), _trunk(x_sample, p))
```

```python
import functools
import math

import jax
import jax.numpy as jnp
from jax import lax
from jax.experimental import pallas as pl
from jax.experimental.pallas import tpu as pltpu

D_MODEL = 1024
BLK = 128
WINDOW = 128
HA = 8
HKV_A = 2
G_A = HA // HKV_A
DH_A = 64
WIDTH_A = HA * DH_A
NUM_BUCKETS = 32
MAX_DISTANCE = 128
HB = 8
Q_LORA = 256
KV_LORA = 128
NOPE = 64
ROPE = 32
DV = 64
WIDTH_B = HB * DV
ROPE_THETA = 10000.0
EPS = 1e-6
NEG = -1e30
LOG2E = math.log2(math.e)
TOKEN_TILE = 512
MLA_Q_TILE = 512
MLA_K_TILE = 256
WIN_QB = 4
MLA_ALL_HEADS_SEQ = 2048
MLA_UNROLL = 64
DV_ROWS = 80
MLA_ROWS = 16

LANES = 128
VMEM_LIMIT = 56 * 1024 * 1024

F32 = jnp.float32
BF16 = jnp.bfloat16

_W1_GROUPS = (("qa", WIDTH_A), ("kva", 2 * HKV_A * DH_A), ("za", WIDTH_A),
              ("cq", Q_LORA), ("ckv", KV_LORA), ("kr", LANES), ("zb", WIDTH_B),
              ("ga", D_MODEL), ("gb", D_MODEL))
_W1_OFF = {}
_o = 0
for _n, _w in _W1_GROUPS:
    _W1_OFF[_n] = (_o, _o + _w)
    _o += _w
W1_COLS = _o


def _pad_heads(w, n_heads, width):
    k = w.shape[0]
    w = w.reshape(k, n_heads, width)
    out = jnp.zeros((k, n_heads, LANES), w.dtype)
    out = out.at[:, :, :width].set(w)
    return out.reshape(k, n_heads * LANES)


def _pack_weights(w_in, w_uq, w_ukv):
    idx = [0]
    for s in (WIDTH_A, HKV_A * DH_A, HKV_A * DH_A, WIDTH_A, Q_LORA, KV_LORA, ROPE, WIDTH_B, D_MODEL, D_MODEL):
        idx.append(idx[-1] + s)
    qa, ka, va, za, cq, ckv, kr, zb, ga, gb = [w_in[:, idx[i]:idx[i + 1]] for i in range(10)]
    kr_p = jnp.zeros((D_MODEL, LANES), w_in.dtype).at[:, NOPE:NOPE + ROPE].set(kr)
    w1 = jnp.concatenate([qa, ka, va, za, cq, ckv, kr_p, zb, ga, gb], axis=1).astype(BF16)
    wq = _pad_heads(w_uq, HB, NOPE + ROPE).astype(BF16)
    assert NOPE + DV == LANES
    wkv = w_ukv.astype(BF16)
    return w1, wq, wkv


def _rope_tables(seq):
    half = ROPE // 2
    inv = jnp.power(ROPE_THETA, -jnp.arange(half, dtype=F32) / half)
    ang = jnp.arange(seq, dtype=F32)[:, None] * inv[None, :]
    cos, sin = jnp.cos(ang), jnp.sin(ang)
    z = jnp.zeros((seq, half), F32)
    pad = jnp.zeros((seq, LANES - NOPE - ROPE), F32)
    ones = jnp.ones((seq, NOPE), F32)
    zn = jnp.zeros((seq, NOPE), F32)
    cosq = jnp.concatenate([ones, cos, cos, pad], axis=1)
    sin_lo = jnp.concatenate([zn, -sin, z, pad], axis=1)
    sin_hi = jnp.concatenate([zn, z, sin, pad], axis=1)
    return cosq, sin_lo, sin_hi


def _window_bias(rel_bias):
    span = 2 * BLK
    rel = jnp.arange(-(span - 1), span, dtype=jnp.int32)
    nb = NUM_BUCKETS // 2
    max_exact = nb // 2
    ret = (rel > 0).astype(jnp.int32) * nb
    n = jnp.abs(rel)
    nf = jnp.maximum(n, 1).astype(F32)
    large = max_exact + (jnp.log(nf / max_exact) / math.log(MAX_DISTANCE / max_exact)
                         * (nb - max_exact)).astype(jnp.int32)
    large = jnp.minimum(large, nb - 1)
    bucket = ret + jnp.where(n < max_exact, n, large)
    line = jnp.where((n <= WINDOW)[:, None], rel_bias.astype(F32)[bucket] * LOG2E, NEG)
    period = 2 * span
    line = jnp.concatenate([line, jnp.zeros((1, HA), F32)], axis=0).T
    skew = jnp.tile(line, (1, BLK))[:, :BLK * (period - 1)].reshape(HA, BLK, period - 1)
    bias = skew[:, :, span - 1 - BLK:span - 1 - BLK + 3 * BLK]
    kj = jnp.arange(3 * BLK, dtype=jnp.int32)[None, None, :]
    variants = []
    for e in range(4):
        gone = ((kj < BLK) & (e % 2 == 1)) | ((kj >= 2 * BLK) & (e // 2 == 1))
        variants.append(jnp.where(gone, NEG, bias))
    return jnp.stack(variants)


def _rms(v, g):
    return v * lax.rsqrt(jnp.mean(v * v, axis=-1, keepdims=True) + EPS) * g


def _sigmoid(v):
    return 1.0 / (1.0 + jnp.exp(-v))


def _rot(v, cosq, sin_lo, sin_hi):
    return v * cosq + pltpu.roll(v, LANES - ROPE // 2, 1) * sin_lo + pltpu.roll(v, ROPE // 2, 1) * sin_hi


def _proj_kernel(x_ref, lng_ref, w1_ref, bg_ref, qg_ref, kvg_ref, wq_ref, wkv_ref,
                 cos_ref, slo_ref, shi_ref,
                 qa_ref, kva_ref, sza_ref, szb_ref, gga_ref, ggb_ref, q_ref, k_ref, v_ref):
    x = x_ref[...]
    h = _rms(x, lng_ref[...]).astype(BF16)

    def proj(name):
        a, b = _W1_OFF[name]
        return jnp.dot(h, w1_ref[:, a:b], preferred_element_type=F32)

    qa_ref[...] = (proj("qa") * (DH_A ** -0.5 * LOG2E)).astype(BF16)
    kva_ref[...] = proj("kva").astype(BF16)
    za = proj("za")
    sza_ref[...] = (za * _sigmoid(za)).astype(BF16)
    zb = proj("zb")
    szb_ref[...] = (zb * _sigmoid(zb)).astype(BF16)
    gga_ref[...] = _sigmoid(proj("ga") + bg_ref[:, :D_MODEL]).astype(BF16)
    ggb_ref[...] = _sigmoid(proj("gb") + bg_ref[:, D_MODEL:]).astype(BF16)

    cosq, slo, shi = cos_ref[...], slo_ref[...], shi_ref[...]
    cqn = _rms(proj("cq"), qg_ref[...]).astype(BF16)
    qf = jnp.dot(cqn, wq_ref[...], preferred_element_type=F32)
    scale = (NOPE + ROPE) ** -0.5 * LOG2E
    for hh in range(HB):
        blk = qf[:, hh * LANES:(hh + 1) * LANES]
        q_ref[hh] = (_rot(blk, cosq, slo, shi) * scale).T.astype(BF16)

    ckvn = _rms(proj("ckv"), kvg_ref[...]).astype(BF16)
    kvf = jnp.dot(ckvn, wkv_ref[...], preferred_element_type=F32)
    krr = _rot(proj("kr"), cosq, slo, shi)
    lane = lax.broadcasted_iota(jnp.int32, (1, LANES), 1)
    ones_col = (lane == DV).astype(F32)
    for hh in range(HB):
        blk = kvf[:, hh * LANES:(hh + 1) * LANES]
        k_ref[hh] = (jnp.where(lane < NOPE, blk, 0.0) + krr).astype(BF16)
        vpad = jnp.where(lane < DV, pltpu.roll(blk, LANES - NOPE, 1), ones_col)
        v_ref[hh] = vpad.T[:DV_ROWS].astype(BF16)


def _proj_call(x2, seq, ln_g, w1, b_gate, q_norm_g, kv_norm_g, wq, wkv, tables, tm):
    t = x2.shape[0]
    nt = t // tm
    per_seq = seq // tm
    row = lambda i: (i, 0)
    const = lambda i: (0, 0)
    tab = lambda i: (i % per_seq, 0)
    head = lambda i: (0, i, 0)
    headt = lambda i: (0, 0, i)
    bs = pl.BlockSpec
    kva = 2 * HKV_A * DH_A
    in_specs = [bs((tm, D_MODEL), row), bs((1, D_MODEL), const), bs((D_MODEL, W1_COLS), const, pipeline_mode=pl.Buffered(1)),
                bs((1, 2 * D_MODEL), const), bs((1, Q_LORA), const), bs((1, KV_LORA), const),
                bs((Q_LORA, HB * LANES), const), bs((KV_LORA, HB * LANES), const),
                bs((tm, LANES), tab), bs((tm, LANES), tab), bs((tm, LANES), tab)]
    out_shape = [jax.ShapeDtypeStruct((t, WIDTH_A), BF16), jax.ShapeDtypeStruct((t, kva), BF16),
                 jax.ShapeDtypeStruct((t, WIDTH_A), BF16), jax.ShapeDtypeStruct((t, WIDTH_B), BF16),
                 jax.ShapeDtypeStruct((t, D_MODEL), BF16), jax.ShapeDtypeStruct((t, D_MODEL), BF16),
                 jax.ShapeDtypeStruct((HB, LANES, t), BF16), jax.ShapeDtypeStruct((HB, t, LANES), BF16),
                 jax.ShapeDtypeStruct((HB, DV_ROWS, t), BF16)]
    out_specs = [bs((tm, WIDTH_A), row), bs((tm, kva), row),
                 bs((tm, WIDTH_A), row), bs((tm, WIDTH_B), row), bs((tm, D_MODEL), row), bs((tm, D_MODEL), row),
                 bs((HB, LANES, tm), headt), bs((HB, tm, LANES), head), bs((HB, DV_ROWS, tm), headt)]
    return pl.pallas_call(
        _proj_kernel, grid=(nt,), in_specs=in_specs, out_specs=out_specs, out_shape=out_shape,
        compiler_params=pltpu.CompilerParams(dimension_semantics=("arbitrary",), vmem_limit_bytes=VMEM_LIMIT),
        name="proj",
    )(x2, ln_g, w1, b_gate, q_norm_g, kv_norm_g, wq, wkv, *tables)


def _window_kernel(nb, sink_ref, bias_ref, q_ref, kvp_ref, kvc_ref, kvn_ref, sza_ref, o_ref, s_ref):
    n = pl.program_id(1)
    qb = q_ref.shape[0] // BLK
    lane = lax.broadcasted_iota(jnp.int32, (1, LANES), 1)

    kv = jnp.concatenate([kvp_ref[...], kvc_ref[...], kvn_ref[...]], axis=0)
    vw = jnp.concatenate([kv[:, HKV_A * DH_A:], jnp.ones((kv.shape[0], LANES), BF16)], axis=1)

    for j in range(qb):
        for hh in range(HA):
            g = hh // G_A
            s_ref[j * HA + hh] = lax.dot_general(
                q_ref[j * BLK:(j + 1) * BLK, hh * DH_A:(hh + 1) * DH_A],
                kv[j * BLK:(j + 3) * BLK, g * DH_A:(g + 1) * DH_A],
                (((1,), (1,)), ((), ())), preferred_element_type=F32)

    for j in range(qb):
        blk = n * qb + j
        edge = (blk == 0).astype(jnp.int32) + 2 * (blk == nb - 1).astype(jnp.int32)
        halves = []
        for hh in range(HA):
            s = s_ref[j * HA + hh] + bias_ref[edge, hh]
            sk = sink_ref[hh] * LOG2E
            m = jnp.maximum(jnp.max(s, axis=-1, keepdims=True), sk)
            p = jnp.exp2(s - m).astype(BF16)
            o = jnp.dot(p, vw[j * BLK:(j + 3) * BLK], preferred_element_type=F32)
            y_h = o[:, :LANES] / (o[:, LANES:] + jnp.exp2(sk - m))
            halves.append(y_h if (hh // G_A) == (hh % 2) else pltpu.roll(y_h, DH_A, 1))
        outs = [jnp.where(lane < DH_A, halves[2 * i], halves[2 * i + 1]) for i in range(HA // 2)]
        y = jnp.concatenate(outs, axis=1)
        rows = slice(j * BLK, (j + 1) * BLK)
        o_ref[rows, :] = (y * sza_ref[rows, :].astype(F32)).astype(BF16)


def _window_call(qa, kva, sza, bias, sink, batch, seq):
    t = qa.shape[0]
    nb = seq // BLK
    qb = min(WIN_QB, nb)
    assert nb % qb == 0
    ns = nb // qb
    tq = qb * BLK
    bs = pl.BlockSpec
    cur = lambda b, n: (b * ns + n, 0)
    prev = lambda b, n: (b * nb + jnp.maximum(n * qb - 1, 0), 0)
    nxt = lambda b, n: (b * nb + jnp.minimum((n + 1) * qb, nb - 1), 0)
    smem = functools.partial(bs, memory_space=pltpu.SMEM)
    kvw = 2 * HKV_A * DH_A
    in_specs = [smem(), bs((4, HA, BLK, 3 * BLK), lambda b, n: (0, 0, 0, 0), pipeline_mode=pl.Buffered(1)),
                bs((tq, WIDTH_A), cur),
                bs((BLK, kvw), prev), bs((tq, kvw), cur), bs((BLK, kvw), nxt),
                bs((tq, WIDTH_A), cur)]
    return pl.pallas_call(
        functools.partial(_window_kernel, nb),
        grid=(batch, ns), in_specs=in_specs, out_specs=bs((tq, WIDTH_A), cur),
        out_shape=jax.ShapeDtypeStruct((t, WIDTH_A), BF16),
        scratch_shapes=[pltpu.VMEM((qb * HA, BLK, 3 * BLK), F32)],
        compiler_params=pltpu.CompilerParams(dimension_semantics=("arbitrary", "arbitrary"),
                                             vmem_limit_bytes=VMEM_LIMIT),
        name="window",
    )(sink, bias, qa, kva, kva, kva, sza)


def _mla_kernel(seq, bk, qt_ref, k_ref, vt_ref, szb_ref, o_ref, s_all, p_all, acc_all):
    bq = qt_ref.shape[2]
    nkv = seq // bk
    unroll = min(MLA_UNROLL, nkv)
    assert unroll % 2 == 0 and nkv % unroll == 0
    n_heads = qt_ref.shape[0]

    def scores(hh, j, slot):
        start = pl.multiple_of(j * bk, bk)
        s = jnp.dot(k_ref[hh, pl.ds(start, bk), :], qt_ref[hh], preferred_element_type=F32)
        s_all[hh, slot] = s
        return jnp.max(s, axis=0, keepdims=True)

    def values(hh, j, slot, alpha):
        start = pl.multiple_of(j * bk, bk)
        pv = jnp.dot(vt_ref[hh, :, pl.ds(start, bk)], p_all[hh, slot], preferred_element_type=F32)
        acc_all[hh] = acc_all[hh] * alpha + pv

    def softmax(hh, slot, m_old, cmax):
        m_new = jnp.maximum(m_old, cmax)
        alpha = jnp.exp2(m_old - m_new)
        mb = jnp.broadcast_to(m_new, (MLA_ROWS, bq))
        for r in range(0, bk, MLA_ROWS):
            rows = slice(r, r + MLA_ROWS)
            p_all[hh, slot, rows, :] = jnp.exp2(s_all[hh, slot, rows, :] - mb).astype(BF16)
        return m_new, alpha

    def prologue(hh):
        cm0 = scores(hh, 0, 0)
        acc_all[hh] = jnp.zeros(acc_all.shape[1:], F32)
        p_all[hh, 1] = jnp.zeros(p_all.shape[2:], BF16)
        return cm0

    def epilogue(hh, alpha):
        values(hh, nkv - 1, 1, alpha)
        pad = jnp.zeros((LANES - DV_ROWS, bq), F32)
        acc = jnp.concatenate([acc_all[hh], pad], axis=0).T
        y = acc[:, :DV] / acc[:, DV:DV + 1]
        cols = slice(hh * DV, (hh + 1) * DV)
        o_ref[:, cols] = (y * szb_ref[:, cols].astype(F32)).astype(BF16)

    cm0 = prologue(0)
    for hh in range(n_heads):
        def group(t, carry, hh=hh):
            m, alpha, cm = carry
            for u in range(unroll):
                j = unroll * t + u
                slot = u % 2
                cm_next = scores(hh, jnp.minimum(j + 1, nkv - 1), 1 - slot)
                values(hh, jnp.maximum(j - 1, 0), 1 - slot, alpha)
                m, alpha = softmax(hh, slot, m, cm)
                cm = cm_next
            return m, alpha, cm

        init = (jnp.full((1, bq), NEG, F32), jnp.ones((1, bq), F32), cm0)
        _, alpha, _ = lax.fori_loop(0, nkv // unroll, group, init)
        if hh + 1 < n_heads:
            cm0 = prologue(hh + 1)
        epilogue(hh, alpha)


def _mla_call(qt, k, vt, szb, batch, seq, bq, bk, hps):
    t = k.shape[1]
    nq = seq // bq
    assert seq % (2 * bk) == 0 and bk % MLA_ROWS == 0 and HB % hps == 0 and hps % 2 == 0
    ow = hps * DV
    bs = pl.BlockSpec
    return pl.pallas_call(
        functools.partial(_mla_kernel, seq, bk),
        grid=(batch, HB // hps, nq),
        in_specs=[bs((hps, LANES, bq), lambda b, hp, i: (hp, 0, b * nq + i)),
                  bs((hps, seq, LANES), lambda b, hp, i: (hp, b, 0)),
                  bs((hps, DV_ROWS, seq), lambda b, hp, i: (hp, 0, b)),
                  bs((bq, ow), lambda b, hp, i: (b * nq + i, hp))],
        out_specs=bs((bq, ow), lambda b, hp, i: (b * nq + i, hp)),
        out_shape=jax.ShapeDtypeStruct((t, WIDTH_B), BF16),
        scratch_shapes=[pltpu.VMEM((hps, 2, bk, bq), F32), pltpu.VMEM((hps, 2, bk, bq), BF16),
                        pltpu.VMEM((hps, DV_ROWS, bq), F32)],
        compiler_params=pltpu.CompilerParams(dimension_semantics=("arbitrary", "arbitrary", "arbitrary"),
                                             vmem_limit_bytes=VMEM_LIMIT),
        name="mla",
    )(qt, k, vt, szb)


def _merge_kernel(x_ref, a_ref, b_ref, gga_ref, ggb_ref, wpa_ref, wpb_ref, wo_ref, fg_ref, o_ref):
    oa = jnp.dot(a_ref[...], wpa_ref[...], preferred_element_type=F32)
    ob = jnp.dot(b_ref[...], wpb_ref[...], preferred_element_type=F32)
    merged = gga_ref[...].astype(F32) * oa + ggb_ref[...].astype(F32) * ob
    y = x_ref[...] + jnp.dot(merged.astype(BF16), wo_ref[...], preferred_element_type=F32)
    o_ref[...] = _rms(y, fg_ref[...])


def _merge_call(x2, a, b, gga, ggb, wpa, wpb, wo, fg, tm):
    t = x2.shape[0]
    bs = pl.BlockSpec
    row = lambda i: (i, 0)
    const = lambda i: (0, 0)
    return pl.pallas_call(
        _merge_kernel, grid=(t // tm,),
        in_specs=[bs((tm, D_MODEL), row), bs((tm, WIDTH_A), row), bs((tm, WIDTH_B), row),
                  bs((tm, D_MODEL), row), bs((tm, D_MODEL), row),
                  bs((WIDTH_A, D_MODEL), const), bs((WIDTH_B, D_MODEL), const), bs((D_MODEL, D_MODEL), const),
                  bs((1, D_MODEL), const)],
        out_specs=bs((tm, D_MODEL), row),
        out_shape=jax.ShapeDtypeStruct((t, D_MODEL), F32),
        compiler_params=pltpu.CompilerParams(dimension_semantics=("arbitrary",), vmem_limit_bytes=VMEM_LIMIT),
        name="merge",
    )(x2, a, b, gga, ggb, wpa, wpb, wo, fg)


def _tiles(seq):
    tm = min(TOKEN_TILE, seq)
    bq = min(MLA_Q_TILE, seq)
    bk = min(MLA_K_TILE, seq // 2)
    hps = HB if seq <= MLA_ALL_HEADS_SEQ else 2
    return tm, bq, bk, hps


def _trunk(x, p):
    batch, seq, _ = x.shape
    tm, bq, bk, hps = _tiles(seq)
    x2 = x.reshape(batch * seq, D_MODEL)
    tables = _rope_tables(seq)
    (qa, kva, sza, szb, gga, ggb, q, k, v) = _proj_call(
        x2, seq, p["ln_g"], p["w1"], p["b_gate"], p["q_norm_g"], p["kv_norm_g"], p["wq"], p["wkv"],
        tables, tm)
    a = _window_call(qa, kva, sza, p["bias"], p["sink"], batch, seq)
    b = _mla_call(q, k, v, szb, batch, seq, bq, bk, hps)
    y = _merge_call(x2, a, b, gga, ggb, p["wpa"], p["wpb"], p["wo"], p["final_g"], tm)
    return y.reshape(batch, seq, D_MODEL)


def kernel(x_prompt, x_sample, ln_g, w_in, b_gate, sink_a, q_norm_g, kv_norm_g, w_uq, w_ukv,
           w_proj_a, w_proj_b, w_out, rel_bias, final_g):
    assert ln_g.shape[0] == 1, "one layer"
    w1, wq, wkv = _pack_weights(w_in[0], w_uq[0], w_ukv[0])
    p = dict(ln_g=ln_g[0][None, :], w1=w1, b_gate=b_gate[0][None, :], q_norm_g=q_norm_g[0][None, :],
             kv_norm_g=kv_norm_g[0][None, :], wq=wq, wkv=wkv,
             bias=_window_bias(rel_bias), sink=sink_a[0].astype(F32),
             wpa=w_proj_a[0].astype(BF16), wpb=w_proj_b[0].astype(BF16), wo=w_out[0].astype(BF16),
             final_g=final_g[None, :])
    return (_trunk(x_prompt, p), _trunk(x_sample, p))
```
